```python
import math
import jax
import jax.numpy as jnp
from jax import lax
import numpy as np

D_MODEL = 2048
BATCH = 2
SEQ = 4096
DEPTH = 2

GRID_W = 64
CTX_LEN = 256

MLA_HEADS = 8
MLA_Q_RANK = 512
MLA_KV_RANK = 256
MLA_NOPE = 128
MLA_ROPE = 64
MLA_V = 128
MLA_SCALE = 1.0 / math.sqrt(MLA_NOPE + MLA_ROPE)
MLA_Q_BLOCK = 128
ROPE_BASE = 10000.0

RG_WIDTH = 1024
RG_BLOCKS = 8
RG_BS = RG_WIDTH // RG_BLOCKS
RG_CONV = 4
RG_C = 8.0

NA_HEADS = 8
NA_HEAD_DIM = 128
NA_WIDTH = NA_HEADS * NA_HEAD_DIM
NA_WIN_ROWS = 8
NA_WIN_COLS = 16
NA_SCALE = 1.0 / math.sqrt(NA_HEAD_DIM)

FFN_HIDDEN = -(-8 * D_MODEL // (3 * 256)) * 256

NORM_EPS = 1e-6
NEG_INF = -1e30

IN_SECTIONS = (MLA_Q_RANK, MLA_KV_RANK, MLA_ROPE, RG_WIDTH, RG_WIDTH, NA_WIDTH, NA_WIDTH, NA_WIDTH, D_MODEL, D_MODEL, D_MODEL)
IN_WIDTH = MLA_Q_RANK + MLA_KV_RANK + MLA_ROPE + 2 * RG_WIDTH + 3 * NA_WIDTH + 3 * D_MODEL

kernel_name = "hybrid_mla_rglru_natten_prefix_dit_block"


def rms_norm(x, g):
    xf = x.astype(jnp.float32)
    y = xf * lax.rsqrt(jnp.mean(xf * xf, axis=-1, keepdims=True) + NORM_EPS)
    return (y * g.astype(jnp.float32)).astype(x.dtype)


def modulate(x, g, shift, scale):
    return rms_norm(x, g) * (1 + scale) + shift


def ada_params(cond, w_ada, b_ada):
    m = jax.nn.silu(cond) @ w_ada + b_ada
    return [t[:, None, :] for t in jnp.split(m, 6, axis=-1)]


def split_in(p):
    offs = np.cumsum(IN_SECTIONS)[:-1].tolist()
    return jnp.split(p, offs, axis=-1)


def axial_rope_tables(n_tok):
    t = jnp.arange(n_tok, dtype=jnp.int32)
    row = (t // GRID_W).astype(jnp.float32)
    col = (t % GRID_W).astype(jnp.float32)
    n_freq = MLA_ROPE // 4
    inv = ROPE_BASE ** (-jnp.arange(n_freq, dtype=jnp.float32) / n_freq)
    ang = jnp.concatenate([row[:, None] * inv, col[:, None] * inv], axis=-1)
    return jnp.cos(ang), jnp.sin(ang)


def apply_axial_rope(x, cos, sin):
    n_freq = MLA_ROPE // 4
    bshape = (1, cos.shape[0]) + (1,) * (x.ndim - 3) + (2, n_freq)
    cs, sn = cos.reshape(bshape), sin.reshape(bshape)
    xf = x.astype(jnp.float32).reshape(x.shape[:-1] + (2, 2, n_freq))
    x1, x2 = xf[..., 0, :], xf[..., 1, :]
    out = jnp.stack([x1 * cs - x2 * sn, x2 * cs + x1 * sn], axis=-2)
    return out.reshape(x.shape).astype(x.dtype)


def mla_qkv(q_a, kv_a, k_rope, g_q_a, w_q_b, g_kv_a, w_kv_b):
    b, n, _ = q_a.shape
    q = (rms_norm(q_a, g_q_a) @ w_q_b).reshape(b, n, MLA_HEADS, MLA_NOPE + MLA_ROPE)
    kv = (rms_norm(kv_a, g_kv_a) @ w_kv_b).reshape(b, n, MLA_HEADS, MLA_NOPE + MLA_V)
    return q[..., :MLA_NOPE], q[..., MLA_NOPE:], kv[..., :MLA_NOPE], k_rope, kv[..., MLA_NOPE:]


def mla_attend(qn, qr, kn, kr, v):
    s = (jnp.einsum('bqhd,bkhd->bhqk', qn, kn, preferred_element_type=jnp.float32)
         + jnp.einsum('bqhr,bkr->bhqk', qr, kr, preferred_element_type=jnp.float32)) * MLA_SCALE
    p = jax.nn.softmax(s, axis=-1).astype(v.dtype)
    return jnp.einsum('bhqk,bkhd->bqhd', p, v)


def mla_latent(qn, qr, kn, kr, v):
    b, s, h, _ = qn.shape
    nblk = s // MLA_Q_BLOCK
    qn_b = qn.reshape(b, nblk, MLA_Q_BLOCK, h, MLA_NOPE).transpose(1, 0, 2, 3, 4)
    qr_b = qr.reshape(b, nblk, MLA_Q_BLOCK, h, MLA_ROPE).transpose(1, 0, 2, 3, 4)
    out = lax.map(lambda a: mla_attend(a[0], a[1], kn, kr, v), (qn_b, qr_b))
    return out.transpose(1, 0, 2, 3, 4).reshape(b, s, h * MLA_V)


def dense_attend(q, k, v, scale):
    s = jnp.einsum('bqhd,bkhd->bhqk', q, k, preferred_element_type=jnp.float32) * scale
    p = jax.nn.softmax(s, axis=-1).astype(v.dtype)
    return jnp.einsum('bhqk,bkhd->bqhd', p, v)


def short_conv(u, w, b):
    pad_l = RG_CONV // 2
    y = lax.conv_general_dilated(u, w[:, None, :].astype(u.dtype), window_strides=(1,),
                                 padding=[(pad_l, RG_CONV - 1 - pad_l)],
                                 dimension_numbers=('NWC', 'WIO', 'NWC'),
                                 feature_group_count=u.shape[-1])
    return y + b


def rglru_coeffs(u, w_a, b_a, w_x, b_x, lam):
    ub = u.reshape(u.shape[:-1] + (RG_BLOCKS, RG_BS))
    r = jax.nn.sigmoid(jnp.einsum('bnhi,hij->bnhj', ub, w_a.astype(jnp.float32)).reshape(u.shape) + b_a.astype(jnp.float32))
    i = jax.nn.sigmoid(jnp.einsum('bnhi,hij->bnhj', ub, w_x.astype(jnp.float32)).reshape(u.shape) + b_x.astype(jnp.float32))
    log_a = -RG_C * r * jax.nn.softplus(-lam.astype(jnp.float32))
    return jnp.exp(log_a), jnp.sqrt(-jnp.expm1(2.0 * log_a)) * (i * u)


def linear_scan(a, b, h0, reverse):
    def combine(left, right):
        return left[0] * right[0], right[0] * left[1] + right[1]
    a_cum, b_cum = lax.associative_scan(combine, (a, b), reverse=reverse, axis=1)
    return a_cum * h0[:, None, :] + b_cum


def bidir_rglru(u_ctx, u_lat, conv_w, conv_b, rg_wa, rg_ba, rg_wx, rg_bx, rg_lambda):
    uc = short_conv(u_ctx, conv_w, conv_b).astype(jnp.float32)
    ul = short_conv(u_lat, conv_w, conv_b).astype(jnp.float32)
    outs_c, outs_l = [], []
    for d, rev in ((0, False), (1, True)):
        a_c, b_c = rglru_coeffs(uc, rg_wa[d], rg_ba[d], rg_wx[d], rg_bx[d], rg_lambda[d])
        h_c = linear_scan(a_c, b_c, jnp.zeros_like(uc[:, 0]), rev)
        h_end = h_c[:, 0] if rev else h_c[:, -1]
        a_l, b_l = rglru_coeffs(ul, rg_wa[d], rg_ba[d], rg_wx[d], rg_bx[d], rg_lambda[d])
        outs_c.append(h_c)
        outs_l.append(linear_scan(a_l, b_l, h_end, rev))
    return (outs_c[0] + outs_c[1]).astype(u_ctx.dtype), (outs_l[0] + outs_l[1]).astype(u_lat.dtype)


def natten_latent(q, k, v, k_ctx, v_ctx, rpb):
    b, n, h, d = q.shape
    rows = n // GRID_W
    win_r = min(NA_WIN_ROWS, rows)
    r = jnp.arange(rows)
    r_idx = jnp.clip(r - win_r // 2, 0, rows - win_r)[:, None] + jnp.arange(win_r)[None, :]
    col = jnp.arange(GRID_W)
    c_start = jnp.clip(col - NA_WIN_COLS // 2, 0, GRID_W - NA_WIN_COLS)
    col_in = (col[None, :] >= c_start[:, None]) & (col[None, :] < c_start[:, None] + NA_WIN_COLS)
    qg = q.reshape(b, rows, GRID_W, h, d)
    kg = k.reshape(b, rows, GRID_W, h, d)[:, r_idx]
    vg = v.reshape(b, rows, GRID_W, h, d)[:, r_idx]
    s_lat = jnp.einsum('brqhd,brkwhd->brhqkw', qg, kg, preferred_element_type=jnp.float32) * NA_SCALE
    dr = r_idx - r[:, None] + (NA_WIN_ROWS - 1)
    dc = jnp.clip(col[None, :] - col[:, None] + (NA_WIN_COLS - 1), 0, 2 * NA_WIN_COLS - 2)
    bias = rpb[:, dr[:, None, :, None], dc[None, :, None, :]].astype(jnp.float32)
    s_lat = jnp.where(col_in[:, None, :], s_lat + bias.transpose(1, 0, 2, 3, 4), NEG_INF)
    s_ctx = jnp.einsum('brqhd,bjhd->brhqj', qg, k_ctx, preferred_element_type=jnp.float32) * NA_SCALE
    n_win = win_r * GRID_W
    s = jnp.concatenate([s_lat.reshape(b, rows, h, GRID_W, n_win), s_ctx], axis=-1)
    p = jax.nn.softmax(s, axis=-1).astype(v.dtype)
    p_lat = p[..., :n_win].reshape(b, rows, h, GRID_W, win_r, GRID_W)
    out = (jnp.einsum('brhqkw,brkwhd->brqhd', p_lat, vg)
           + jnp.einsum('brhqj,bjhd->brqhd', p[..., n_win:], v_ctx))
    return out.reshape(b, n, h * d)


def merge_branches(g_mla, g_rg, g_na, mla_o, rg_o, na_o, w_mla_o, w_rg_o, w_na_o, w_out):
    y = (jax.nn.sigmoid(g_mla) * (mla_o @ w_mla_o)
         + jax.nn.sigmoid(g_rg) * (rg_o @ w_rg_o)
         + jax.nn.sigmoid(g_na) * (na_o @ w_na_o))
    return y @ w_out


def ffn_sublayer(x, shift, scale, gate, g_pre, g_post, w_gate, w_up, w_down):
    h = modulate(x, g_pre, shift, scale)
    y = (jax.nn.silu(h @ w_gate) * (h @ w_up)) @ w_down
    return x + gate * rms_norm(y, g_post)


def trunk_layer(xc, xl, mods_c, mods_l, cos, sin, with_ctx_out,
                g_mix_pre, g_mix_post, g_ffn_pre, g_ffn_post, w_in, g_q_a, w_q_b, g_kv_a, w_kv_b,
                w_mla_o, conv_w, conv_b, rg_wa, rg_ba, rg_wx, rg_bx, rg_lambda, w_rg_o,
                na_rpb, w_na_o, w_out, w_ffn_gate, w_ffn_up, w_ffn_down):
    sh1c, sc1c, gt1c, sh2c, sc2c, gt2c = mods_c
    sh1l, sc1l, gt1l, sh2l, sc2l, gt2l = mods_l
    b, n_ctx, _ = xc.shape
    n_lat = xl.shape[1]
    pc = split_in(modulate(xc, g_mix_pre, sh1c, sc1c) @ w_in)
    pl = split_in(modulate(xl, g_mix_pre, sh1l, sc1l) @ w_in)
    qn_c, qr_c, kn_c, kr_c, v_c = mla_qkv(pc[0], pc[1], pc[2], g_q_a, w_q_b, g_kv_a, w_kv_b)
    qn_l, qr_l, kn_l, kr_l, v_l = mla_qkv(pl[0], pl[1], pl[2], g_q_a, w_q_b, g_kv_a, w_kv_b)
    qr_l = apply_axial_rope(qr_l, cos, sin)
    kr_l = apply_axial_rope(kr_l, cos, sin)
    mla_l = mla_latent(qn_l, qr_l, jnp.concatenate([kn_c, kn_l], axis=1),
                       jnp.concatenate([kr_c, kr_l], axis=1), jnp.concatenate([v_c, v_l], axis=1))
    rec_c, rec_l = bidir_rglru(pc[3], pl[3], conv_w, conv_b, rg_wa, rg_ba, rg_wx, rg_bx, rg_lambda)
    rg_l = jax.nn.gelu(pl[4]) * rec_l
    na_shape_c = (b, n_ctx, NA_HEADS, NA_HEAD_DIM)
    na_shape_l = (b, n_lat, NA_HEADS, NA_HEAD_DIM)
    nk_c, nv_c = pc[6].reshape(na_shape_c), pc[7].reshape(na_shape_c)
    na_l = natten_latent(pl[5].reshape(na_shape_l), pl[6].reshape(na_shape_l), pl[7].reshape(na_shape_l),
                         nk_c, nv_c, na_rpb)
    yl = merge_branches(pl[8], pl[9], pl[10], mla_l, rg_l, na_l, w_mla_o, w_rg_o, w_na_o, w_out)
    xl = xl + gt1l * rms_norm(yl, g_mix_post)
    xl = ffn_sublayer(xl, sh2l, sc2l, gt2l, g_ffn_pre, g_ffn_post, w_ffn_gate, w_ffn_up, w_ffn_down)
    if with_ctx_out:
        mla_c = mla_attend(qn_c, qr_c, kn_c, kr_c, v_c).reshape(b, n_ctx, MLA_HEADS * MLA_V)
        rg_c = jax.nn.gelu(pc[4]) * rec_c
        na_c = dense_attend(pc[5].reshape(na_shape_c), nk_c, nv_c, NA_SCALE).reshape(b, n_ctx, NA_WIDTH)
        yc = merge_branches(pc[8], pc[9], pc[10], mla_c, rg_c, na_c, w_mla_o, w_rg_o, w_na_o, w_out)
        xc = xc + gt1c * rms_norm(yc, g_mix_post)
        xc = ffn_sublayer(xc, sh2c, sc2c, gt2c, g_ffn_pre, g_ffn_post, w_ffn_gate, w_ffn_up, w_ffn_down)
    return xc, xl


def setup_inputs(seed: int = 0) -> dict:
    key = jax.random.key(seed)
    ks = jax.random.split(key, 40)
    counter = [0]

    def nk():
        k = ks[counter[0]]
        counter[0] += 1
        return k

    def nrm(shape, scale):
        return jax.random.normal(nk(), shape, jnp.float32) * scale

    def gain(shape):
        return 1.0 + nrm(shape, 0.02)

    L, D = DEPTH, D_MODEL
    x = nrm((BATCH, SEQ, D), 1.0)
    c = nrm((BATCH, D), 1.0)
    ctx = nrm((BATCH, CTX_LEN, D), 1.0)
    c_ctx = nrm((D,), 1.0)
    w_ada = nrm((L, D, 6 * D), D ** -0.5)
    b_ada = nrm((L, 6 * D), 0.01)
    g_mix_pre = gain((L, D))
    g_mix_post = gain((L, D))
    g_ffn_pre = gain((L, D))
    g_ffn_post = gain((L, D))
    w_in = nrm((L, D, IN_WIDTH), D ** -0.5)
    g_q_a = gain((L, MLA_Q_RANK))
    w_q_b = nrm((L, MLA_Q_RANK, MLA_HEADS * (MLA_NOPE + MLA_ROPE)), MLA_Q_RANK ** -0.5)
    g_kv_a = gain((L, MLA_KV_RANK))
    w_kv_b = nrm((L, MLA_KV_RANK, MLA_HEADS * (MLA_NOPE + MLA_V)), MLA_KV_RANK ** -0.5)
    w_mla_o = nrm((L, MLA_HEADS * MLA_V, D), (MLA_HEADS * MLA_V) ** -0.5)
    conv_w = nrm((L, RG_CONV, RG_WIDTH), RG_CONV ** -0.5)
    conv_b = nrm((L, RG_WIDTH), 0.01)
    rg_wa = nrm((L, 2, RG_BLOCKS, RG_BS, RG_BS), RG_BS ** -0.5)
    rg_ba = nrm((L, 2, RG_WIDTH), 0.01)
    rg_wx = nrm((L, 2, RG_BLOCKS, RG_BS, RG_BS), RG_BS ** -0.5)
    rg_bx = nrm((L, 2, RG_WIDTH), 0.01)
    u = jax.random.uniform(nk(), (L, 2, RG_WIDTH), jnp.float32, 0.9, 0.999)
    a_base = u ** (1.0 / RG_C)
    rg_lambda = jnp.log(a_base) - jnp.log1p(-a_base)
    w_rg_o = nrm((L, RG_WIDTH, D), RG_WIDTH ** -0.5)
    na_rpb = nrm((L, NA_HEADS, 2 * NA_WIN_ROWS - 1, 2 * NA_WIN_COLS - 1), 0.1)
    w_na_o = nrm((L, NA_WIDTH, D), NA_WIDTH ** -0.5)
    w_out = nrm((L, D, D), D ** -0.5)
    w_ffn_gate = nrm((L, D, FFN_HIDDEN), D ** -0.5)
    w_ffn_up = nrm((L, D, FFN_HIDDEN), D ** -0.5)
    w_ffn_down = nrm((L, FFN_HIDDEN, D), FFN_HIDDEN ** -0.5)
    return {"x": x, "c": c, "ctx": ctx, "c_ctx": c_ctx, "w_ada": w_ada, "b_ada": b_ada,
            "g_mix_pre": g_mix_pre, "g_mix_post": g_mix_post, "g_ffn_pre": g_ffn_pre, "g_ffn_post": g_ffn_post,
            "w_in": w_in, "g_q_a": g_q_a, "w_q_b": w_q_b, "g_kv_a": g_kv_a, "w_kv_b": w_kv_b,
            "w_mla_o": w_mla_o, "conv_w": conv_w, "conv_b": conv_b, "rg_wa": rg_wa, "rg_ba": rg_ba,
            "rg_wx": rg_wx, "rg_bx": rg_bx, "rg_lambda": rg_lambda, "w_rg_o": w_rg_o,
            "na_rpb": na_rpb, "w_na_o": w_na_o, "w_out": w_out,
            "w_ffn_gate": w_ffn_gate, "w_ffn_up": w_ffn_up, "w_ffn_down": w_ffn_down}


def reference(x, c, ctx, c_ctx, w_ada, b_ada, g_mix_pre, g_mix_post, g_ffn_pre, g_ffn_post,
              w_in, g_q_a, w_q_b, g_kv_a, w_kv_b, w_mla_o, conv_w, conv_b, rg_wa, rg_ba,
              rg_wx, rg_bx, rg_lambda, w_rg_o, na_rpb, w_na_o, w_out, w_ffn_gate, w_ffn_up, w_ffn_down):
    cos, sin = axial_rope_tables(x.shape[1])
    xc, xl = ctx, x
    for i in range(DEPTH):
        mods_l = ada_params(c, w_ada[i], b_ada[i])
        mods_c = ada_params(c_ctx[None, :], w_ada[i], b_ada[i])
        xc, xl = trunk_layer(
            xc, xl, mods_c, mods_l, cos, sin, i < DEPTH - 1,
            g_mix_pre=g_mix_pre[i], g_mix_post=g_mix_post[i], g_ffn_pre=g_ffn_pre[i], g_ffn_post=g_ffn_post[i],
            w_in=w_in[i], g_q_a=g_q_a[i], w_q_b=w_q_b[i], g_kv_a=g_kv_a[i], w_kv_b=w_kv_b[i],
            w_mla_o=w_mla_o[i], conv_w=conv_w[i], conv_b=conv_b[i], rg_wa=rg_wa[i], rg_ba=rg_ba[i],
            rg_wx=rg_wx[i], rg_bx=rg_bx[i], rg_lambda=rg_lambda[i], w_rg_o=w_rg_o[i],
            na_rpb=na_rpb[i], w_na_o=w_na_o[i], w_out=w_out[i],
            w_ffn_gate=w_ffn_gate[i], w_ffn_up=w_ffn_up[i], w_ffn_down=w_ffn_down[i])
    return xl
```

```python
import functools
import math

import numpy as np
import jax
import jax.numpy as jnp
from jax import lax
from jax.experimental import pallas as pl
from jax.experimental.pallas import tpu as pltpu

F32 = jnp.float32
BF16 = jnp.bfloat16

D_MODEL = 2048
GRID_W = 64
MLA_HEADS = 8
MLA_Q_RANK = 512
MLA_KV_RANK = 256
MLA_NOPE = 128
MLA_ROPE = 64
MLA_V = 128
MLA_QK = MLA_NOPE + MLA_ROPE
MLA_SCALE = 1.0 / math.sqrt(MLA_NOPE + MLA_ROPE)
ROPE_BASE = 10000.0
RG_WIDTH = 1024
RG_BLOCKS = 8
RG_BS = RG_WIDTH // RG_BLOCKS
RG_CONV = 4
RG_C = 8.0
NA_HEADS = 8
NA_HEAD_DIM = 128
NA_WIDTH = NA_HEADS * NA_HEAD_DIM
NA_WIN_ROWS = 8
NA_WIN_COLS = 16
NA_SCALE = 1.0 / math.sqrt(NA_HEAD_DIM)
NORM_EPS = 1e-6
NEG_INF = -1e30

SUBLANES = 8
ROW_TILE = 512
COND_ROWS = 8
VMEM_LIMIT = 48 * 1024 * 1024

NT_DIMS = (((1,), (1,)), ((), ()))


def _cparams(*sem):
    return pltpu.CompilerParams(dimension_semantics=sem, vmem_limit_bytes=VMEM_LIMIT)


def _rms(x, g):
    return x * lax.rsqrt(jnp.mean(x * x, axis=-1, keepdims=True) + NORM_EPS) * g


def _ada_kernel(c_ref, w_ref, b_ref, o_ref):
    c = c_ref[...]
    s = (c * jax.nn.sigmoid(c)).astype(BF16)
    o_ref[...] = jnp.dot(s, w_ref[...].astype(BF16), preferred_element_type=F32) + b_ref[...]


def _ada(cond, w_ada, b_ada):
    depth, d, n = w_ada.shape
    tn = 1024
    return pl.pallas_call(
        _ada_kernel,
        grid=(depth, n // tn),
        in_specs=[pl.BlockSpec((COND_ROWS, d), lambda l, j: (0, 0)),
                  pl.BlockSpec((None, d, tn), lambda l, j: (l, 0, j)),
                  pl.BlockSpec((None, 1, tn), lambda l, j: (l, 0, j))],
        out_specs=pl.BlockSpec((None, COND_ROWS, tn), lambda l, j: (l, 0, j)),
        out_shape=jax.ShapeDtypeStruct((depth, COND_ROWS, n), F32),
        compiler_params=_cparams("arbitrary", "arbitrary"),
        name="ada",
    )(cond, w_ada, b_ada.reshape(depth, 1, n))


def _mod_spec(chunk, cond_row):
    return pl.BlockSpec((None, None, 1, D_MODEL), lambda i, *_: (cond_row(i), chunk, 0, 0))


def _modulate_kernel(x_ref, g_ref, sh_ref, sc_ref, o_ref):
    y = _rms(x_ref[...], g_ref[...])
    o_ref[...] = (y * (1.0 + sc_ref[...]) + sh_ref[...]).astype(o_ref.dtype)


def _modulate(x, g, mods, k_shift, k_scale, cond_row):
    m, d = x.shape
    tm = ROW_TILE
    return pl.pallas_call(
        _modulate_kernel,
        grid=(m // tm,),
        in_specs=[pl.BlockSpec((tm, d), lambda i: (i, 0)),
                  pl.BlockSpec((1, d), lambda i: (0, 0)),
                  _mod_spec(k_shift, cond_row),
                  _mod_spec(k_scale, cond_row)],
        out_specs=pl.BlockSpec((tm, d), lambda i: (i, 0)),
        out_shape=jax.ShapeDtypeStruct((m, d), BF16),
        compiler_params=_cparams("arbitrary"),
        name="modulate",
    )(x, g.reshape(1, d), mods, mods)


def _mm_kernel(a_ref, b_ref, o_ref, *, act):
    acc = jnp.dot(a_ref[...], b_ref[...], preferred_element_type=F32)
    if act == "sigmoid":
        acc = jax.nn.sigmoid(acc)
    o_ref[...] = acc.astype(o_ref.dtype)


def _matmul(a, b, out_dtype, tn, act=None, name="matmul"):
    m, k = a.shape
    n = b.shape[1]
    tm = ROW_TILE
    return pl.pallas_call(
        functools.partial(_mm_kernel, act=act),
        grid=(n // tn, m // tm),
        in_specs=[pl.BlockSpec((tm, k), lambda j, i: (i, 0)),
                  pl.BlockSpec((k, tn), lambda j, i: (0, j))],
        out_specs=pl.BlockSpec((tm, tn), lambda j, i: (i, j)),
        out_shape=jax.ShapeDtypeStruct((m, n), out_dtype),
        compiler_params=_cparams("arbitrary", "arbitrary"),
        name=name,
    )(a, b)


def _mla_qkv_kernel(a_ref, gq_ref, gkv_ref, wq_ref, wkv_ref, c_ref, s_ref, q_ref, k_ref, v_ref):
    a = a_ref[...]
    qa = _rms(a[:, :MLA_Q_RANK], gq_ref[...]).astype(BF16)
    kva = _rms(a[:, MLA_Q_RANK:MLA_Q_RANK + MLA_KV_RANK], gkv_ref[...]).astype(BF16)
    q = jnp.dot(qa, wq_ref[...], preferred_element_type=F32)
    kv = jnp.dot(kva, wkv_ref[...], preferred_element_type=F32)
    cos = c_ref[...]
    sin = s_ref[...]
    o_kr = MLA_Q_RANK + MLA_KV_RANK
    kr = (a[:, o_kr:o_kr + MLA_ROPE] * cos + a[:, o_kr + MLA_ROPE:o_kr + 2 * MLA_ROPE] * sin).astype(BF16)
    n_nope = MLA_HEADS * MLA_NOPE
    n_rope = MLA_HEADS * MLA_ROPE
    for h in range(MLA_HEADS):
        q_ref[h, :, 0:MLA_NOPE] = q[:, h * MLA_NOPE:(h + 1) * MLA_NOPE].astype(BF16)
        r0 = n_nope + h * MLA_ROPE
        qr = q[:, r0:r0 + MLA_ROPE] * cos + q[:, r0 + n_rope:r0 + n_rope + MLA_ROPE] * sin
        q_ref[h, :, MLA_NOPE:MLA_QK] = qr.astype(BF16)
        k_ref[h, :, 0:MLA_NOPE] = kv[:, h * MLA_NOPE:(h + 1) * MLA_NOPE].astype(BF16)
        k_ref[h, :, MLA_NOPE:MLA_QK] = kr
    v_ref[...] = kv[:, n_nope:].astype(BF16)


def _mla_qkv(a, gq, gkv, wq, wkv, cos, sin):
    m, wa = a.shape
    tm = ROW_TILE
    nt = cos.shape[0] // tm
    hv = MLA_HEADS * MLA_V
    return pl.pallas_call(
        _mla_qkv_kernel,
        grid=(m // tm,),
        in_specs=[pl.BlockSpec((tm, wa), lambda i: (i, 0)),
                  pl.BlockSpec((1, MLA_Q_RANK), lambda i: (0, 0)),
                  pl.BlockSpec((1, MLA_KV_RANK), lambda i: (0, 0)),
                  pl.BlockSpec(wq.shape, lambda i: (0, 0)),
                  pl.BlockSpec(wkv.shape, lambda i: (0, 0)),
                  pl.BlockSpec((tm, MLA_ROPE), lambda i: (i % nt, 0)),
                  pl.BlockSpec((tm, MLA_ROPE), lambda i: (i % nt, 0))],
        out_specs=[pl.BlockSpec((MLA_HEADS, tm, MLA_QK), lambda i: (0, i, 0)),
                   pl.BlockSpec((MLA_HEADS, tm, MLA_QK), lambda i: (0, i, 0)),
                   pl.BlockSpec((tm, hv), lambda i: (i, 0))],
        out_shape=[jax.ShapeDtypeStruct((MLA_HEADS, m, MLA_QK), BF16),
                   jax.ShapeDtypeStruct((MLA_HEADS, m, MLA_QK), BF16),
                   jax.ShapeDtypeStruct((m, hv), BF16)],
        compiler_params=_cparams("arbitrary"),
        name="mla_qkv",
    )(a, gq.reshape(1, -1), gkv.reshape(1, -1), wq, wkv, cos, sin)


def _attn_kernel(*refs, nseg):
    q_ref = refs[0]
    k_refs = refs[1:1 + nseg]
    v_refs = refs[1 + nseg:1 + 2 * nseg]
    o_ref = refs[1 + 2 * nseg]
    q = q_ref[...]
    scores = [lax.dot_general(q, k[...], NT_DIMS, preferred_element_type=F32) for k in k_refs]
    m = scores[0].max(axis=-1, keepdims=True)
    for s in scores[1:]:
        m = jnp.maximum(m, s.max(axis=-1, keepdims=True))
    den = None
    acc = None
    for s, v in zip(scores, v_refs):
        p = jnp.exp(s - m)
        ps = p.sum(axis=-1, keepdims=True)
        pv = jnp.dot(p.astype(BF16), v[...], preferred_element_type=F32)
        den = ps if den is None else den + ps
        acc = pv if acc is None else acc + pv
    o_ref[...] = (acc / den).astype(o_ref.dtype)


def _attend(grid, q, q_spec, ks, k_specs, vs, v_specs, out_shape, out_spec, name):
    nseg = len(ks)
    return pl.pallas_call(
        functools.partial(_attn_kernel, nseg=nseg),
        grid=grid,
        in_specs=[q_spec] + list(k_specs) + list(v_specs),
        out_specs=out_spec,
        out_shape=out_shape,
        compiler_params=_cparams(*(["arbitrary"] * len(grid))),
        name=name,
    )(q, *ks, *vs)


def _natten_kernel(q_ref, k_ref, v_ref, kc_ref, vc_ref, bias_ref, o_ref, *, rows_per_step, n_rows):
    rb = pl.program_id(2)
    kc = kc_ref[...]
    vc = vc_ref[...]
    n_win = NA_WIN_ROWS * GRID_W

    def row(j, carry):
        r = rb * rows_per_step + j
        start = jnp.clip(r - NA_WIN_ROWS // 2, 0, n_rows - NA_WIN_ROWS)
        rel = start - r + (NA_WIN_ROWS - 1)
        q0 = pl.multiple_of(j * GRID_W, GRID_W)
        k0 = pl.multiple_of(start * GRID_W, GRID_W)
        q = q_ref[pl.ds(q0, GRID_W), :]
        kw = k_ref[pl.ds(k0, n_win), :]
        vw = v_ref[pl.ds(k0, n_win), :]
        s_l = lax.dot_general(q, kw, NT_DIMS, preferred_element_type=F32) + bias_ref[rel]
        s_c = lax.dot_general(q, kc, NT_DIMS, preferred_element_type=F32)
        m = jnp.maximum(s_l.max(axis=-1, keepdims=True), s_c.max(axis=-1, keepdims=True))
        p_l = jnp.exp(s_l - m)
        p_c = jnp.exp(s_c - m)
        den = p_l.sum(axis=-1, keepdims=True) + p_c.sum(axis=-1, keepdims=True)
        acc = (jnp.dot(p_l.astype(BF16), vw, preferred_element_type=F32)
               + jnp.dot(p_c.astype(BF16), vc, preferred_element_type=F32))
        o_ref[pl.ds(q0, GRID_W), :] = (acc / den).astype(o_ref.dtype)
        return carry

    lax.fori_loop(0, rows_per_step, row, 0)


def _natten(na_l, na_c, bias, batch, seq, ctx_len):
    n_rows = seq // GRID_W
    rps = 8
    nrb = n_rows // rps
    tq = rps * GRID_W
    hd = NA_HEAD_DIM
    nh = NA_HEADS
    return pl.pallas_call(
        functools.partial(_natten_kernel, rows_per_step=rps, n_rows=n_rows),
        grid=(batch, nh, nrb),
        in_specs=[pl.BlockSpec((tq, hd), lambda b, h, r: (b * nrb + r, h)),
                  pl.BlockSpec((seq, hd), lambda b, h, r: (b, nh + h)),
                  pl.BlockSpec((seq, hd), lambda b, h, r: (b, 2 * nh + h)),
                  pl.BlockSpec((ctx_len, hd), lambda b, h, r: (b, nh + h)),
                  pl.BlockSpec((ctx_len, hd), lambda b, h, r: (b, 2 * nh + h)),
                  pl.BlockSpec((None, NA_WIN_ROWS, GRID_W, NA_WIN_ROWS * GRID_W), lambda b, h, r: (h, 0, 0, 0))],
        out_specs=pl.BlockSpec((tq, hd), lambda b, h, r: (b * nrb + r, h)),
        out_shape=jax.ShapeDtypeStruct((batch * seq, NA_WIDTH), BF16),
        compiler_params=_cparams("arbitrary", "arbitrary", "arbitrary"),
        name="natten",
    )(na_l, na_l, na_l, na_c, na_c, bias)


def _natten_bias(rpb):
    col = np.arange(GRID_W)
    c_start = np.clip(col - NA_WIN_COLS // 2, 0, GRID_W - NA_WIN_COLS)
    col_in = (col[None, :] >= c_start[:, None]) & (col[None, :] < c_start[:, None] + NA_WIN_COLS)
    dc = np.clip(col[None, :] - col[:, None] + (NA_WIN_COLS - 1), 0, 2 * NA_WIN_COLS - 2)
    dr = np.arange(NA_WIN_ROWS)[:, None] + np.arange(NA_WIN_ROWS)[None, :]
    t = rpb[:, dr[:, None, :, None], dc[None, :, None, :]].astype(F32)
    t = jnp.where(jnp.asarray(col_in)[None, None, :, None, :], t, NEG_INF)
    return t.reshape(rpb.shape[0], NA_WIN_ROWS, GRID_W, NA_WIN_ROWS * GRID_W)


def _softplus(x):
    return jnp.maximum(x, 0.0) + jnp.log1p(jnp.exp(-jnp.abs(x)))


def _rg_kernel(ul_ref, uc_ref, gl_ref, gc_ref, cw_ref, cb_ref, wa_ref, ba_ref, wx_ref, bx_ref, lam_ref,
               ol_ref, oc_ref, upad, ucv, a_s, b_s, h_s, *, t_lat, t_ctx):
    pad = SUBLANES
    cw = cw_ref[...]
    cb = cb_ref[...]
    zeros_pad = jnp.zeros((pad, RG_BS), F32)

    def conv_stream(u_ref, t_len, dst, chunk):
        upad[0:pad, :] = zeros_pad
        upad[pad:pad + t_len, :] = u_ref[...]
        upad[pad + t_len:2 * pad + t_len, :] = zeros_pad
        for c0 in range(0, t_len, chunk):
            y = cb
            for k in range(RG_CONV):
                y = y + cw[k:k + 1, :] * upad[pad + c0 + k - RG_CONV // 2:pad + c0 + k - RG_CONV // 2 + chunk, :]
            ucv[dst + c0:dst + c0 + chunk, :] = y

    conv_stream(uc_ref, t_ctx, 0, t_ctx)
    conv_stream(ul_ref, t_lat, t_ctx, 512)

    sp = _softplus(-lam_ref[...])
    coef_chunk = 256

    def coef(c, carry):
        r0 = pl.multiple_of(c * coef_chunk, coef_chunk)
        u = ucv[pl.ds(r0, coef_chunk), :]
        ub = u.astype(BF16)
        for d in range(2):
            r = jax.nn.sigmoid(jnp.dot(ub, wa_ref[d], preferred_element_type=F32) + ba_ref[d:d + 1, :])
            i = jax.nn.sigmoid(jnp.dot(ub, wx_ref[d], preferred_element_type=F32) + bx_ref[d:d + 1, :])
            log_a = (-RG_C) * r * sp[d:d + 1, :]
            a = jnp.exp(log_a)
            a_s[d, pl.ds(r0, coef_chunk), :] = a
            b_s[d, pl.ds(r0, coef_chunk), :] = jnp.sqrt((1.0 - a) * (1.0 + a)) * (i * u)
        return carry

    lax.fori_loop(0, (t_ctx + t_lat) // coef_chunk, coef, 0)

    sub = lax.broadcasted_iota(jnp.int32, (SUBLANES, RG_BS), 0)

    def scan_stream(off, t_len, h0_f, h0_b):
        seg = t_len // SUBLANES

        def ld(ref, d, j):
            return ref[d, pl.ds(off + j, SUBLANES, stride=seg), :]

        def sweep1(j, st):
            hf, pf, hb, pb = st
            jb = seg - 1 - j
            af = ld(a_s, 0, j)
            ab = ld(a_s, 1, jb)
            return (af * hf + ld(b_s, 0, j), af * pf, ab * hb + ld(b_s, 1, jb), ab * pb)

        z = jnp.zeros((SUBLANES, RG_BS), F32)
        o = jnp.ones((SUBLANES, RG_BS), F32)
        hf, pf, hb, pb = lax.fori_loop(0, seg, sweep1, (z, o, z, o))

        start_f = z
        c = h0_f
        for s in range(SUBLANES):
            start_f = jnp.where(sub == s, c, start_f)
            c = pf[s:s + 1, :] * c + hf[s:s + 1, :]
        end_f = c
        start_b = z
        c = h0_b
        for s in range(SUBLANES - 1, -1, -1):
            start_b = jnp.where(sub == s, c, start_b)
            c = pb[s:s + 1, :] * c + hb[s:s + 1, :]
        end_b = c

        def sweep2(j, st):
            hf, hb = st
            jb = seg - 1 - j
            hf = ld(a_s, 0, j) * hf + ld(b_s, 0, j)
            hb = ld(a_s, 1, jb) * hb + ld(b_s, 1, jb)
            h_s[0, pl.ds(off + j, SUBLANES, stride=seg), :] = hf
            h_s[1, pl.ds(off + jb, SUBLANES, stride=seg), :] = hb
            return (hf, hb)

        lax.fori_loop(0, seg, sweep2, (start_f, start_b))
        return end_f, end_b

    zero_row = jnp.zeros((1, RG_BS), F32)
    end_f, end_b = scan_stream(0, t_ctx, zero_row, zero_row)
    scan_stream(t_ctx, t_lat, end_f, end_b)

    oc_ref[...] = (jax.nn.gelu(gc_ref[...]) * (h_s[0, 0:t_ctx, :] + h_s[1, 0:t_ctx, :])).astype(oc_ref.dtype)
    out_chunk = 512

    def emit(c, carry):
        r0 = pl.multiple_of(c * out_chunk, out_chunk)
        rec = h_s[0, pl.ds(t_ctx + r0, out_chunk), :] + h_s[1, pl.ds(t_ctx + r0, out_chunk), :]
        ol_ref[pl.ds(r0, out_chunk), :] = (jax.nn.gelu(gl_ref[pl.ds(r0, out_chunk), :]) * rec).astype(ol_ref.dtype)
        return carry

    lax.fori_loop(0, t_lat // out_chunk, emit, 0)


def _rglru(rg_l, rg_c, conv_w, conv_b, wa, ba, wx, bx, lam, batch, seq, ctx_len):
    nb = RG_BLOCKS
    bs = RG_BS
    t_all = seq + ctx_len
    blk = lambda b, j: (0, j)
    return pl.pallas_call(
        functools.partial(_rg_kernel, t_lat=seq, t_ctx=ctx_len),
        grid=(batch, nb),
        in_specs=[pl.BlockSpec((seq, bs), lambda b, j: (b, j)),
                  pl.BlockSpec((ctx_len, bs), lambda b, j: (b, j)),
                  pl.BlockSpec((seq, bs), lambda b, j: (b, nb + j)),
                  pl.BlockSpec((ctx_len, bs), lambda b, j: (b, nb + j)),
                  pl.BlockSpec((RG_CONV, bs), blk),
                  pl.BlockSpec((1, bs), blk),
                  pl.BlockSpec((2, None, bs, bs), lambda b, j: (0, j, 0, 0)),
                  pl.BlockSpec((2, bs), blk),
                  pl.BlockSpec((2, None, bs, bs), lambda b, j: (0, j, 0, 0)),
                  pl.BlockSpec((2, bs), blk),
                  pl.BlockSpec((2, bs), blk)],
        out_specs=[pl.BlockSpec((seq, bs), lambda b, j: (b, j)),
                   pl.BlockSpec((ctx_len, bs), lambda b, j: (b, j))],
        out_shape=[jax.ShapeDtypeStruct((batch * seq, RG_WIDTH), BF16),
                   jax.ShapeDtypeStruct((batch * ctx_len, RG_WIDTH), BF16)],
        scratch_shapes=[pltpu.VMEM((seq + 2 * SUBLANES, bs), F32),
                        pltpu.VMEM((t_all, bs), F32),
                        pltpu.VMEM((2, t_all, bs), F32),
                        pltpu.VMEM((2, t_all, bs), F32),
                        pltpu.VMEM((2, t_all, bs), F32)],
        compiler_params=_cparams("arbitrary", "arbitrary"),
        name="rglru",
    )(rg_l, rg_c, rg_l, rg_c, conv_w, conv_b.reshape(1, -1), wa, ba, wx, bx, lam)


def _merge_kernel(m_ref, r_ref, n_ref, gm_ref, gr_ref, gn_ref, wm_ref, wr_ref, wn_ref, o_ref):
    y = gm_ref[...].astype(F32) * jnp.dot(m_ref[...], wm_ref[...], preferred_element_type=F32)
    y = y + gr_ref[...].astype(F32) * jnp.dot(r_ref[...], wr_ref[...], preferred_element_type=F32)
    y = y + gn_ref[...].astype(F32) * jnp.dot(n_ref[...], wn_ref[...], preferred_element_type=F32)
    o_ref[...] = y.astype(o_ref.dtype)


def _merge(mla_o, rg_o, na_o, gates, wm, wr, wn):
    m, k = mla_o.shape
    n = wm.shape[1]
    tm = ROW_TILE
    tn = 1024
    nj = n // tn
    a_spec = pl.BlockSpec((tm, k), lambda j, i: (i, 0))
    w_spec = pl.BlockSpec((k, tn), lambda j, i: (0, j))
    return pl.pallas_call(
        _merge_kernel,
        grid=(nj, m // tm),
        in_specs=[a_spec, a_spec, a_spec,
                  pl.BlockSpec((tm, tn), lambda j, i: (i, j)),
                  pl.BlockSpec((tm, tn), lambda j, i: (i, nj + j)),
                  pl.BlockSpec((tm, tn), lambda j, i: (i, 2 * nj + j)),
                  w_spec, w_spec, w_spec],
        out_specs=pl.BlockSpec((tm, tn), lambda j, i: (i, j)),
        out_shape=jax.ShapeDtypeStruct((m, n), BF16),
        compiler_params=_cparams("arbitrary", "arbitrary"),
        name="merge",
    )(mla_o, rg_o, na_o, gates, gates, gates, wm, wr, wn)


def _proj_res_kernel(y_ref, w_ref, x_ref, g_ref, gate_ref, o_ref):
    z = jnp.dot(y_ref[...], w_ref[...], preferred_element_type=F32)
    o_ref[...] = x_ref[...] + gate_ref[...] * _rms(z, g_ref[...])


def _proj_residual(y, w, x, g, mods, k_gate, cond_row):
    m, d = x.shape
    tm = ROW_TILE
    return pl.pallas_call(
        _proj_res_kernel,
        grid=(m // tm,),
        in_specs=[pl.BlockSpec((tm, y.shape[1]), lambda i: (i, 0)),
                  pl.BlockSpec(w.shape, lambda i: (0, 0)),
                  pl.BlockSpec((tm, d), lambda i: (i, 0)),
                  pl.BlockSpec((1, d), lambda i: (0, 0)),
                  _mod_spec(k_gate, cond_row)],
        out_specs=pl.BlockSpec((tm, d), lambda i: (i, 0)),
        out_shape=jax.ShapeDtypeStruct((m, d), F32),
        compiler_params=_cparams("arbitrary"),
        name="out_proj",
    )(y, w, x, g.reshape(1, d), mods)


def _ffn_up_kernel(h_ref, wg_ref, wu_ref, o_ref):
    h = h_ref[...]
    g = jnp.dot(h, wg_ref[...], preferred_element_type=F32)
    u = jnp.dot(h, wu_ref[...], preferred_element_type=F32)
    o_ref[...] = (g * jax.nn.sigmoid(g) * u).astype(o_ref.dtype)


def _ffn_up(h, wg, wu):
    m, k = h.shape
    n = wg.shape[1]
    tm = ROW_TILE
    tn = 512
    w_spec = pl.BlockSpec((k, tn), lambda j, i: (0, j))
    return pl.pallas_call(
        _ffn_up_kernel,
        grid=(n // tn, m // tm),
        in_specs=[pl.BlockSpec((tm, k), lambda j, i: (i, 0)), w_spec, w_spec],
        out_specs=pl.BlockSpec((tm, tn), lambda j, i: (i, j)),
        out_shape=jax.ShapeDtypeStruct((m, n), BF16),
        compiler_params=_cparams("arbitrary", "arbitrary"),
        name="ffn_up",
    )(h, wg, wu)


def _ffn_down_kernel(a_ref, w_ref, x_ref, g_ref, gate_ref, o_ref, acc_ref):
    kk = pl.program_id(1)

    @pl.when(kk == 0)
    def _():
        acc_ref[...] = jnp.zeros_like(acc_ref)

    acc_ref[...] += jnp.dot(a_ref[...], w_ref[...], preferred_element_type=F32)

    @pl.when(kk == pl.num_programs(1) - 1)
    def _():
        o_ref[...] = x_ref[...] + gate_ref[...] * _rms(acc_ref[...], g_ref[...])


def _ffn_down(act, w, x, g, mods, k_gate, cond_row):
    m, d = x.shape
    kdim = act.shape[1]
    tm = ROW_TILE
    tk = 512
    return pl.pallas_call(
        _ffn_down_kernel,
        grid=(m // tm, kdim // tk),
        in_specs=[pl.BlockSpec((tm, tk), lambda i, k: (i, k)),
                  pl.BlockSpec((tk, d), lambda i, k: (k, 0)),
                  pl.BlockSpec((tm, d), lambda i, k: (i, 0)),
                  pl.BlockSpec((1, d), lambda i, k: (0, 0)),
                  _mod_spec(k_gate, cond_row)],
        out_specs=pl.BlockSpec((tm, d), lambda i, k: (i, 0)),
        out_shape=jax.ShapeDtypeStruct((m, d), F32),
        scratch_shapes=[pltpu.VMEM((tm, d), F32)],
        compiler_params=_cparams("arbitrary", "arbitrary"),
        name="ffn_down",
    )(act, w, x, g.reshape(1, d), mods)


def _rope_tables(n_tok):
    t = jnp.arange(n_tok, dtype=jnp.int32)
    row = (t // GRID_W).astype(F32)
    col = (t % GRID_W).astype(F32)
    n_freq = MLA_ROPE // 4
    inv = ROPE_BASE ** (-jnp.arange(n_freq, dtype=F32) / n_freq)
    ang = jnp.stack([row[:, None] * inv, col[:, None] * inv], axis=1)
    cos = jnp.broadcast_to(jnp.cos(ang)[:, :, None, :], (n_tok, 2, 2, n_freq))
    sin = jnp.sin(ang)
    sin = jnp.stack([-sin, sin], axis=2)
    return cos.reshape(n_tok, MLA_ROPE), sin.reshape(n_tok, MLA_ROPE)


def _half_swap_perm():
    n_freq = MLA_ROPE // 4
    return np.arange(MLA_ROPE).reshape(2, 2, n_freq)[:, ::-1, :].reshape(MLA_ROPE)


def _layer_weights(w_in, w_q_b, w_kv_b):
    perm = _half_swap_perm()
    o = np.cumsum((0, MLA_Q_RANK, MLA_KV_RANK, MLA_ROPE, RG_WIDTH, RG_WIDTH, NA_WIDTH, NA_WIDTH, NA_WIDTH,
                   D_MODEL, D_MODEL, D_MODEL))
    w_a = jnp.concatenate([w_in[:, :o[3]], w_in[:, o[2]:o[3]][:, perm]], axis=1).astype(BF16)
    w_rg = w_in[:, o[3]:o[5]].astype(BF16)
    w_na = jnp.concatenate([w_in[:, o[5]:o[6]] * NA_SCALE, w_in[:, o[6]:o[8]]], axis=1).astype(BF16)
    w_gt = w_in[:, o[8]:o[11]].astype(BF16)
    wq = w_q_b.reshape(MLA_Q_RANK, MLA_HEADS, MLA_QK)
    wq_rope = wq[:, :, MLA_NOPE:]
    wq = jnp.concatenate([wq[:, :, :MLA_NOPE].reshape(MLA_Q_RANK, -1),
                          wq_rope.reshape(MLA_Q_RANK, -1),
                          wq_rope[:, :, perm].reshape(MLA_Q_RANK, -1)], axis=1)
    wq = (wq * MLA_SCALE).astype(BF16)
    wkv = w_kv_b.reshape(MLA_KV_RANK, MLA_HEADS, MLA_NOPE + MLA_V)
    wkv = jnp.concatenate([wkv[:, :, :MLA_NOPE].reshape(MLA_KV_RANK, -1),
                           wkv[:, :, MLA_NOPE:].reshape(MLA_KV_RANK, -1)], axis=1).astype(BF16)
    return w_a, w_rg, w_na, w_gt, wq, wkv


def kernel(x, c, ctx, c_ctx, w_ada, b_ada, g_mix_pre, g_mix_post, g_ffn_pre, g_ffn_post, w_in, g_q_a, w_q_b, g_kv_a, w_kv_b, w_mla_o, conv_w, conv_b, rg_wa, rg_ba, rg_wx, rg_bx, rg_lambda, w_rg_o, na_rpb, w_na_o, w_out, w_ffn_gate, w_ffn_up, w_ffn_down):
    batch, seq, d = x.shape
    ctx_len = ctx.shape[1]
    depth = w_ada.shape[0]
    assert d == D_MODEL and batch + 1 <= COND_ROWS
    assert seq % ROW_TILE == 0 and (batch * ctx_len) % ROW_TILE == 0 and seq % GRID_W == 0
    lat_row = lambda i: i // (seq // ROW_TILE)
    ctx_row = lambda i: batch
    hq, hv = MLA_HEADS, MLA_HEADS * MLA_V

    cond = jnp.zeros((COND_ROWS, d), F32).at[:batch].set(c).at[batch].set(c_ctx)
    mods = _ada(cond, w_ada, b_ada).reshape(depth, COND_ROWS, 6, 1, d)

    cos_l, sin_l = _rope_tables(seq)
    cos_c = jnp.ones((batch * ctx_len, MLA_ROPE), F32)
    sin_c = jnp.zeros((batch * ctx_len, MLA_ROPE), F32)

    xl = x.reshape(batch * seq, d)
    xc = ctx.reshape(batch * ctx_len, d)
    nq = seq // ROW_TILE

    for l in range(depth):
        last = l == depth - 1
        md = mods[l]
        w_a, w_rg, w_na, w_gt, wq, wkv = _layer_weights(w_in[l], w_q_b[l], w_kv_b[l])

        hl = _modulate(xl, g_mix_pre[l], md, 0, 1, lat_row)
        hc = _modulate(xc, g_mix_pre[l], md, 0, 1, ctx_row)
        a_l = _matmul(hl, w_a, F32, w_a.shape[1], name="proj_mla")
        a_c = _matmul(hc, w_a, F32, w_a.shape[1], name="proj_mla")
        rg_l = _matmul(hl, w_rg, F32, 1024, name="proj_rg")
        rg_c = _matmul(hc, w_rg, F32, 1024, name="proj_rg")
        na_l = _matmul(hl, w_na, BF16, 1024, name="proj_na")
        na_c = _matmul(hc, w_na, BF16, 1024, name="proj_na")
        gt_l = _matmul(hl, w_gt, BF16, 1024, act="sigmoid", name="proj_gate")

        q_l, k_l, v_l = _mla_qkv(a_l, g_q_a[l], g_kv_a[l], wq, wkv, cos_l, sin_l)
        q_c, k_c, v_c = _mla_qkv(a_c, g_q_a[l], g_kv_a[l], wq, wkv, cos_c, sin_c)
        mla_l = _attend(
            (batch, hq, nq), q_l,
            pl.BlockSpec((None, ROW_TILE, MLA_QK), lambda b, h, i: (h, b * nq + i, 0)),
            [k_l, k_c],
            [pl.BlockSpec((None, seq, MLA_QK), lambda b, h, i: (h, b, 0)),
             pl.BlockSpec((None, ctx_len, MLA_QK), lambda b, h, i: (h, b, 0))],
            [v_l, v_c],
            [pl.BlockSpec((seq, MLA_V), lambda b, h, i: (b, h)),
             pl.BlockSpec((ctx_len, MLA_V), lambda b, h, i: (b, h))],
            jax.ShapeDtypeStruct((batch * seq, hv), BF16),
            pl.BlockSpec((ROW_TILE, MLA_V), lambda b, h, i: (b * nq + i, h)),
            "mla_attn")

        rgo_l, rgo_c = _rglru(rg_l, rg_c, conv_w[l], conv_b[l], rg_wa[l].astype(BF16), rg_ba[l],
                              rg_wx[l].astype(BF16), rg_bx[l], rg_lambda[l], batch, seq, ctx_len)

        nao_l = _natten(na_l, na_c, _natten_bias(na_rpb[l]), batch, seq, ctx_len)

        wm, wr, wn, wo = (w_mla_o[l].astype(BF16), w_rg_o[l].astype(BF16), w_na_o[l].astype(BF16),
                          w_out[l].astype(BF16))
        wfg, wfu, wfd = w_ffn_gate[l].astype(BF16), w_ffn_up[l].astype(BF16), w_ffn_down[l].astype(BF16)

        def tail(xs, mla_o, rg_o, na_o, gates, t):
            y = _merge(mla_o, rg_o, na_o, gates, wm, wr, wn)
            xs = _proj_residual(y, wo, xs, g_mix_post[l], md, 2, t)
            h2 = _modulate(xs, g_ffn_pre[l], md, 3, 4, t)
            act = _ffn_up(h2, wfg, wfu)
            return _ffn_down(act, wfd, xs, g_ffn_post[l], md, 5, t)

        xl = tail(xl, mla_l, rgo_l, nao_l, gt_l, lat_row)

        if not last:
            gt_c = _matmul(hc, w_gt, BF16, 1024, act="sigmoid", name="proj_gate")
            mla_c = _attend(
                (batch, hq), q_c,
                pl.BlockSpec((None, ctx_len, MLA_QK), lambda b, h: (h, b, 0)),
                [k_c], [pl.BlockSpec((None, ctx_len, MLA_QK), lambda b, h: (h, b, 0))],
                [v_c], [pl.BlockSpec((ctx_len, MLA_V), lambda b, h: (b, h))],
                jax.ShapeDtypeStruct((batch * ctx_len, hv), BF16),
                pl.BlockSpec((ctx_len, MLA_V), lambda b, h: (b, h)),
                "mla_attn_ctx")
            nh = NA_HEADS
            nao_c = _attend(
                (batch, nh), na_c,
                pl.BlockSpec((ctx_len, NA_HEAD_DIM), lambda b, h: (b, h)),
                [na_c], [pl.BlockSpec((ctx_len, NA_HEAD_DIM), lambda b, h: (b, nh + h))],
                [na_c], [pl.BlockSpec((ctx_len, NA_HEAD_DIM), lambda b, h: (b, 2 * nh + h))],
                jax.ShapeDtypeStruct((batch * ctx_len, NA_WIDTH), BF16),
                pl.BlockSpec((ctx_len, NA_HEAD_DIM), lambda b, h: (b, h)),
                "na_attn_ctx")
            xc = tail(xc, mla_c, rgo_c, nao_c, gt_c, ctx_row)

    return xl.reshape(batch, seq, d)
```

```python
import functools
import math

import numpy as np
import jax
import jax.numpy as jnp
from jax import lax
from jax.experimental import pallas as pl
from jax.experimental.pallas import tpu as pltpu

F32 = jnp.float32
BF16 = jnp.bfloat16

D_MODEL = 2048
GRID_W = 64
MLA_HEADS = 8
MLA_Q_RANK = 512
MLA_KV_RANK = 256
MLA_NOPE = 128
MLA_ROPE = 64
MLA_V = 128
MLA_QK = MLA_NOPE + MLA_ROPE
MLA_SCALE = 1.0 / math.sqrt(MLA_NOPE + MLA_ROPE)
ROPE_BASE = 10000.0
RG_WIDTH = 1024
RG_BLOCKS = 8
RG_BS = RG_WIDTH // RG_BLOCKS
RG_CONV = 4
RG_C = 8.0
NA_HEADS = 8
NA_HEAD_DIM = 128
NA_WIDTH = NA_HEADS * NA_HEAD_DIM
NA_WIN_ROWS = 8
NA_WIN_COLS = 16
NA_SCALE = 1.0 / math.sqrt(NA_HEAD_DIM)
NORM_EPS = 1e-6
NEG_INF = -1e30
LOG2E = math.log2(math.e)

SUBLANES = 8
ROW_TILE = 512
KEY_CHUNK = 512
COND_ROWS = 8
NA_ROWS_PER_STEP = 8
NA_KEY_ROWS = 16
NA_KEY_BLOCK = 4
SEG_PAD = 4
VMEM_LIMIT = 48 * 1024 * 1024

NT_DIMS = (((1,), (1,)), ((), ()))


def _cparams(*sem):
    return pltpu.CompilerParams(dimension_semantics=sem, vmem_limit_bytes=VMEM_LIMIT)


def _rms(x, g):
    return x * lax.rsqrt(jnp.mean(x * x, axis=-1, keepdims=True) + NORM_EPS) * g


def _sigmoid(x):
    return 0.5 * jnp.tanh(0.5 * x) + 0.5


def _ada_kernel(c_ref, w_ref, b_ref, o_ref):
    c = c_ref[...]
    s = (c * jax.nn.sigmoid(c)).astype(BF16)
    o_ref[...] = jnp.dot(s, w_ref[...].astype(BF16), preferred_element_type=F32) + b_ref[...]


def _ada(cond, w_ada, b_ada):
    depth, d, n = w_ada.shape
    tn = 1024
    return pl.pallas_call(
        _ada_kernel,
        grid=(depth, n // tn),
        in_specs=[pl.BlockSpec((COND_ROWS, d), lambda l, j: (0, 0)),
                  pl.BlockSpec((None, d, tn), lambda l, j: (l, 0, j)),
                  pl.BlockSpec((None, 1, tn), lambda l, j: (l, 0, j))],
        out_specs=pl.BlockSpec((None, COND_ROWS, tn), lambda l, j: (l, 0, j)),
        out_shape=jax.ShapeDtypeStruct((depth, COND_ROWS, n), F32),
        compiler_params=_cparams("arbitrary", "arbitrary"),
        name="ada",
    )(cond, w_ada, b_ada.reshape(depth, 1, n))


def _mod_spec(chunk, cond_row):
    return pl.BlockSpec((None, None, 1, D_MODEL), lambda i, *_: (cond_row(i), chunk, 0, 0))


def _modulate_kernel(x_ref, g_ref, sh_ref, sc_ref, o_ref):
    y = _rms(x_ref[...], g_ref[...])
    o_ref[...] = (y * (1.0 + sc_ref[...]) + sh_ref[...]).astype(o_ref.dtype)


def _modulate(x, g, mods, k_shift, k_scale, cond_row):
    m, d = x.shape
    tm = ROW_TILE
    return pl.pallas_call(
        _modulate_kernel,
        grid=(m // tm,),
        in_specs=[pl.BlockSpec((tm, d), lambda i: (i, 0)),
                  pl.BlockSpec((1, d), lambda i: (0, 0)),
                  _mod_spec(k_shift, cond_row),
                  _mod_spec(k_scale, cond_row)],
        out_specs=pl.BlockSpec((tm, d), lambda i: (i, 0)),
        out_shape=jax.ShapeDtypeStruct((m, d), BF16),
        compiler_params=_cparams("arbitrary"),
        name="modulate",
    )(x, g.reshape(1, d), mods, mods)


def _mm_kernel(a_ref, b_ref, o_ref, *, act):
    acc = jnp.dot(a_ref[...], b_ref[...], preferred_element_type=F32)
    if act == "sigmoid":
        acc = _sigmoid(acc)
    o_ref[...] = acc.astype(o_ref.dtype)


def _matmul(a, b, out_dtype, tn, act=None, name="matmul"):
    m, k = a.shape
    n = b.shape[1]
    tm = ROW_TILE
    return pl.pallas_call(
        functools.partial(_mm_kernel, act=act),
        grid=(n // tn, m // tm),
        in_specs=[pl.BlockSpec((tm, k), lambda j, i: (i, 0)),
                  pl.BlockSpec((k, tn), lambda j, i: (0, j))],
        out_specs=pl.BlockSpec((tm, tn), lambda j, i: (i, j)),
        out_shape=jax.ShapeDtypeStruct((m, n), out_dtype),
        compiler_params=_cparams("arbitrary", "arbitrary"),
        name=name,
    )(a, b)


def _mm_nt_kernel(w_ref, x_ref, o_ref):
    o_ref[...] = lax.dot_general(w_ref[...], x_ref[...], NT_DIMS, preferred_element_type=F32).astype(o_ref.dtype)


def _matmul_nt(wt, x, out_dtype, tn, name):
    n, k = wt.shape
    m = x.shape[0]
    tm = ROW_TILE
    return pl.pallas_call(
        _mm_nt_kernel,
        grid=(n // tn, m // tm),
        in_specs=[pl.BlockSpec((tn, k), lambda j, i: (j, 0)),
                  pl.BlockSpec((tm, k), lambda j, i: (i, 0))],
        out_specs=pl.BlockSpec((tn, tm), lambda j, i: (j, i)),
        out_shape=jax.ShapeDtypeStruct((n, m), out_dtype),
        compiler_params=_cparams("arbitrary", "arbitrary"),
        name=name,
    )(wt, x)


def _mla_qkv_kernel(a_ref, gq_ref, gkv_ref, wq_ref, wk_ref, wvt_ref, c_ref, s_ref, q_ref, k_ref, vt_ref):
    a = a_ref[...]
    qa = _rms(a[:, :MLA_Q_RANK], gq_ref[...]).astype(BF16)
    kva = _rms(a[:, MLA_Q_RANK:MLA_Q_RANK + MLA_KV_RANK], gkv_ref[...]).astype(BF16)
    q = jnp.dot(qa, wq_ref[...], preferred_element_type=F32)
    kn = jnp.dot(kva, wk_ref[...], preferred_element_type=F32)
    vt_ref[...] = lax.dot_general(wvt_ref[...], kva, NT_DIMS, preferred_element_type=F32).astype(BF16)
    cos = c_ref[...]
    sin = s_ref[...]
    o_kr = MLA_Q_RANK + MLA_KV_RANK
    kr = (a[:, o_kr:o_kr + MLA_ROPE] * cos + a[:, o_kr + MLA_ROPE:o_kr + 2 * MLA_ROPE] * sin).astype(BF16)
    n_nope = MLA_HEADS * MLA_NOPE
    n_rope = MLA_HEADS * MLA_ROPE
    for h in range(MLA_HEADS):
        q_ref[h, :, 0:MLA_NOPE] = q[:, h * MLA_NOPE:(h + 1) * MLA_NOPE].astype(BF16)
        r0 = n_nope + h * MLA_ROPE
        qr = q[:, r0:r0 + MLA_ROPE] * cos + q[:, r0 + n_rope:r0 + n_rope + MLA_ROPE] * sin
        q_ref[h, :, MLA_NOPE:MLA_QK] = qr.astype(BF16)
        k_ref[h, :, 0:MLA_NOPE] = kn[:, h * MLA_NOPE:(h + 1) * MLA_NOPE].astype(BF16)
        k_ref[h, :, MLA_NOPE:MLA_QK] = kr


def _mla_qkv(a, gq, gkv, wq, wk, wvt, cos, sin):
    m, wa = a.shape
    tm = ROW_TILE
    nt = cos.shape[0] // tm
    hv = MLA_HEADS * MLA_V
    return pl.pallas_call(
        _mla_qkv_kernel,
        grid=(m // tm,),
        in_specs=[pl.BlockSpec((tm, wa), lambda i: (i, 0)),
                  pl.BlockSpec((1, MLA_Q_RANK), lambda i: (0, 0)),
                  pl.BlockSpec((1, MLA_KV_RANK), lambda i: (0, 0)),
                  pl.BlockSpec(wq.shape, lambda i: (0, 0)),
                  pl.BlockSpec(wk.shape, lambda i: (0, 0)),
                  pl.BlockSpec(wvt.shape, lambda i: (0, 0)),
                  pl.BlockSpec((tm, MLA_ROPE), lambda i: (i % nt, 0)),
                  pl.BlockSpec((tm, MLA_ROPE), lambda i: (i % nt, 0))],
        out_specs=[pl.BlockSpec((MLA_HEADS, tm, MLA_QK), lambda i: (0, i, 0)),
                   pl.BlockSpec((MLA_HEADS, tm, MLA_QK), lambda i: (0, i, 0)),
                   pl.BlockSpec((hv, tm), lambda i: (0, i))],
        out_shape=[jax.ShapeDtypeStruct((MLA_HEADS, m, MLA_QK), BF16),
                   jax.ShapeDtypeStruct((MLA_HEADS, m, MLA_QK), BF16),
                   jax.ShapeDtypeStruct((hv, m), BF16)],
        compiler_params=_cparams("arbitrary"),
        name="mla_qkv",
    )(a, gq.reshape(1, -1), gkv.reshape(1, -1), wq, wk, wvt, cos, sin)


def _flash_t(q, chunks):
    scores = []
    m = None
    for k, _, bias_t in chunks:
        s = lax.dot_general(k, q, NT_DIMS, preferred_element_type=F32)
        if bias_t is not None:
            s = s + bias_t
        scores.append(s)
        mc = s.max(axis=0, keepdims=True)
        m = mc if m is None else jnp.maximum(m, mc)
    den = acc = None
    for s, (_, vt, _) in zip(scores, chunks):
        p = jnp.exp2(s - m)
        ps = p.sum(axis=0, keepdims=True)
        pv = jnp.dot(vt, p.astype(BF16), preferred_element_type=F32)
        den = ps if den is None else den + ps
        acc = pv if acc is None else acc + pv
    return acc / den


def _attn_kernel(*refs, plan, nseg):
    q_ref = refs[0]
    k_refs = refs[1:1 + nseg]
    vt_refs = refs[1 + nseg:1 + 2 * nseg]
    o_ref = refs[1 + 2 * nseg]
    chunks = [(k_refs[s][c0:c0 + n, :], vt_refs[s][:, c0:c0 + n], None) for s, c0, n in plan]
    o_ref[...] = _flash_t(q_ref[...], chunks).T.astype(o_ref.dtype)


def _attend(grid, plan, q, q_spec, ks, k_specs, vts, vt_specs, out_shape, out_spec, name):
    nseg = len(ks)
    return pl.pallas_call(
        functools.partial(_attn_kernel, plan=tuple(plan), nseg=nseg),
        grid=grid,
        in_specs=[q_spec] + list(k_specs) + list(vt_specs),
        out_specs=out_spec,
        out_shape=out_shape,
        compiler_params=_cparams(*(["arbitrary"] * len(grid))),
        name=name,
    )(q, *ks, *vts)


def _natten_row_structure(n_rows):
    def block(r0):
        ws = np.clip(r0 - NA_WIN_ROWS // 2, 0, n_rows - NA_KEY_ROWS)
        r = r0 + np.arange(NA_ROWS_PER_STEP)
        start = np.clip(r - NA_WIN_ROWS // 2, 0, n_rows - NA_WIN_ROWS)
        key_row = ws + np.arange(NA_KEY_ROWS)
        kr = key_row[None, :] - start[:, None]
        valid = (kr >= 0) & (kr < NA_WIN_ROWS)
        dr = key_row[None, :] - r[:, None] + (NA_WIN_ROWS - 1)
        return valid, np.where(valid, dr, 0)

    blocks = [block(r0) for r0 in range(0, n_rows, NA_ROWS_PER_STEP)]
    for valid, dr in blocks[2:-1]:
        assert (valid == blocks[1][0]).all() and (dr == blocks[1][1]).all()
    return blocks[0], blocks[1], blocks[-1]


def _natten_kernel(q_ref, k0, k1, k2, k3, v0, v1, v2, v3, kc_ref, vc_ref, rpb_ref, o_ref, bias, *, structure):
    rb = pl.program_id(2)

    @pl.when((pl.program_id(1) == 0) & (rb == 0))
    def _():
        masked = jnp.full((GRID_W, GRID_W), NEG_INF, F32)
        for v, (valid, dr) in enumerate(structure):
            for i in range(NA_KEY_ROWS):
                tiles = [rpb_ref[int(dr[j, i])] if valid[j, i] else masked for j in range(NA_ROWS_PER_STEP)]
                bias[v, i * GRID_W:(i + 1) * GRID_W, :] = jnp.concatenate(tiles, axis=1)

    variant = jnp.where(rb == 0, 0, jnp.where(rb == pl.num_programs(2) - 1, 2, 1))
    half = bias.shape[1] // 2
    chunks = [
        (kc_ref[...], vc_ref[...], None),
        (jnp.concatenate([k0[...], k1[...]], axis=0), jnp.concatenate([v0[...], v1[...]], axis=1),
         bias[variant, 0:half, :]),
        (jnp.concatenate([k2[...], k3[...]], axis=0), jnp.concatenate([v2[...], v3[...]], axis=1),
         bias[variant, half:2 * half, :]),
    ]
    o_ref[...] = _flash_t(q_ref[...], chunks).T.astype(o_ref.dtype)


def _natten(qk_l, vt_l, qk_c, vt_c, rpb_tiles, batch, seq, ctx_len):
    n_rows = seq // GRID_W
    nrb = n_rows // NA_ROWS_PER_STEP
    tq = NA_ROWS_PER_STEP * GRID_W
    kb = NA_KEY_BLOCK * GRID_W
    nkb = seq // kb
    hd = NA_HEAD_DIM
    nh = NA_HEADS
    last_block = (n_rows - NA_KEY_ROWS) // NA_KEY_BLOCK

    def win(r):
        return jnp.clip(2 * r - 1, 0, last_block)

    k_specs = [pl.BlockSpec((kb, hd), lambda h, b, r, c=c: (b * nkb + win(r) + c, nh + h)) for c in range(4)]
    v_specs = [pl.BlockSpec((hd, kb), lambda h, b, r, c=c: (h, b * nkb + win(r) + c)) for c in range(4)]
    return pl.pallas_call(
        functools.partial(_natten_kernel, structure=_natten_row_structure(n_rows)),
        grid=(nh, batch, nrb),
        in_specs=[pl.BlockSpec((tq, hd), lambda h, b, r: (b * nrb + r, h))] + k_specs + v_specs + [
            pl.BlockSpec((ctx_len, hd), lambda h, b, r: (b, nh + h)),
            pl.BlockSpec((hd, ctx_len), lambda h, b, r: (h, b)),
            pl.BlockSpec((None,) + rpb_tiles.shape[1:], lambda h, b, r: (h, 0, 0, 0))],
        out_specs=pl.BlockSpec((tq, hd), lambda h, b, r: (b * nrb + r, h)),
        out_shape=jax.ShapeDtypeStruct((batch * seq, NA_WIDTH), BF16),
        scratch_shapes=[pltpu.VMEM((3, NA_KEY_ROWS * GRID_W, tq), F32)],
        compiler_params=_cparams("arbitrary", "arbitrary", "arbitrary"),
        name="natten",
    )(qk_l, *([qk_l] * 4), *([vt_l] * 4), qk_c, vt_c, rpb_tiles)


def _natten_rpb_tiles(rpb):
    col = np.arange(GRID_W)
    c_start = np.clip(col - NA_WIN_COLS // 2, 0, GRID_W - NA_WIN_COLS)
    col_in = (col[None, :] >= c_start[:, None]) & (col[None, :] < c_start[:, None] + NA_WIN_COLS)
    dc = np.clip(col[None, :] - col[:, None] + (NA_WIN_COLS - 1), 0, 2 * NA_WIN_COLS - 2)
    onehot = (dc.T[None] == np.arange(2 * NA_WIN_COLS - 1)[:, None, None]).astype(np.float32)
    t = jnp.einsum("hdc,ckq->hdkq", rpb.astype(F32), onehot, precision=lax.Precision.HIGHEST)
    return jnp.where(col_in.T[None, None], t * LOG2E, NEG_INF)


def _softplus(x):
    return jnp.maximum(x, 0.0) + jnp.log1p(jnp.exp(-jnp.abs(x)))


def _rg_kernel(ul_ref, uc_ref, gl_ref, gc_ref, cw_ref, cb_ref, wa_ref, ba_ref, wx_ref, bx_ref, lam_ref,
               ol_ref, oc_ref, upad, ucv, a_s, b_s, h_s, *, t_lat, t_ctx):
    pad = SUBLANES
    cw = cw_ref[...]
    cb = cb_ref[...]
    zeros_pad = jnp.zeros((pad, RG_BS), F32)
    seg_c = t_ctx // SUBLANES + SEG_PAD
    seg_l = t_lat // SUBLANES + SEG_PAD
    off_l = SUBLANES * seg_c

    def conv_stream(u_ref, t_len, dst, chunk):
        upad[0:pad, :] = zeros_pad
        upad[pad:pad + t_len, :] = u_ref[...]
        upad[pad + t_len:2 * pad + t_len, :] = zeros_pad
        for c0 in range(0, t_len, chunk):
            y = cb
            for k in range(RG_CONV):
                y = y + cw[k:k + 1, :] * upad[pad + c0 + k - RG_CONV // 2:pad + c0 + k - RG_CONV // 2 + chunk, :]
            ucv[dst + c0:dst + c0 + chunk, :] = y

    conv_stream(uc_ref, t_ctx, 0, t_ctx)
    conv_stream(ul_ref, t_lat, t_ctx, 512)

    rate = (-RG_C * LOG2E) * _softplus(-lam_ref[...])
    coef_chunk = 256

    def coef_range(src0, dst0, n_chunks):
        def coef(c, carry):
            r0 = pl.multiple_of(src0 + c * coef_chunk, SUBLANES)
            w0 = pl.multiple_of(dst0 + c * coef_chunk, SUBLANES)
            u = ucv[pl.ds(r0, coef_chunk), :]
            ub = u.astype(BF16)
            for d in range(2):
                r = _sigmoid(jnp.dot(ub, wa_ref[d], preferred_element_type=F32) + ba_ref[d:d + 1, :])
                i = _sigmoid(jnp.dot(ub, wx_ref[d], preferred_element_type=F32) + bx_ref[d:d + 1, :])
                a = jnp.exp2(r * rate[d:d + 1, :])
                a_s[d, pl.ds(w0, coef_chunk), :] = a
                b_s[d, pl.ds(w0, coef_chunk), :] = jnp.sqrt((1.0 - a) * (1.0 + a)) * (i * u)
            return carry
        lax.fori_loop(0, n_chunks, coef, 0)

    coef_range(0, 0, t_ctx // coef_chunk)
    coef_range(t_ctx, off_l, t_lat // coef_chunk)
    for d in range(2):
        for lo, hi in ((t_ctx, off_l), (off_l + t_lat, off_l + SUBLANES * seg_l)):
            a_s[d, lo:hi, :] = jnp.ones((hi - lo, RG_BS), F32)
            b_s[d, lo:hi, :] = jnp.zeros((hi - lo, RG_BS), F32)

    sub = lax.broadcasted_iota(jnp.int32, (SUBLANES, RG_BS), 0)

    def scan_stream(off, seg, h0_f, h0_b):
        def ld(ref, d, j):
            return ref[d, pl.ds(off + j, SUBLANES, stride=seg), :]

        def sweep1(j, st):
            hf, pf, hb, pb = st
            jb = seg - 1 - j
            af = ld(a_s, 0, j)
            ab = ld(a_s, 1, jb)
            return (af * hf + ld(b_s, 0, j), af * pf, ab * hb + ld(b_s, 1, jb), ab * pb)

        z = jnp.zeros((SUBLANES, RG_BS), F32)
        o = jnp.ones((SUBLANES, RG_BS), F32)
        hf, pf, hb, pb = lax.fori_loop(0, seg, sweep1, (z, o, z, o), unroll=4)

        start_f = z
        c = h0_f
        for s in range(SUBLANES):
            start_f = jnp.where(sub == s, c, start_f)
            c = pf[s:s + 1, :] * c + hf[s:s + 1, :]
        end_f = c
        start_b = z
        c = h0_b
        for s in range(SUBLANES - 1, -1, -1):
            start_b = jnp.where(sub == s, c, start_b)
            c = pb[s:s + 1, :] * c + hb[s:s + 1, :]
        end_b = c

        def sweep2(j, st):
            hf, hb = st
            jb = seg - 1 - j
            hf = ld(a_s, 0, j) * hf + ld(b_s, 0, j)
            hb = ld(a_s, 1, jb) * hb + ld(b_s, 1, jb)
            h_s[0, pl.ds(off + j, SUBLANES, stride=seg), :] = hf
            h_s[1, pl.ds(off + jb, SUBLANES, stride=seg), :] = hb
            return (hf, hb)

        lax.fori_loop(0, seg, sweep2, (start_f, start_b), unroll=4)
        return end_f, end_b

    zero_row = jnp.zeros((1, RG_BS), F32)
    end_f, end_b = scan_stream(0, seg_c, zero_row, zero_row)
    scan_stream(off_l, seg_l, end_f, end_b)

    oc_ref[...] = (jax.nn.gelu(gc_ref[...]) * (h_s[0, 0:t_ctx, :] + h_s[1, 0:t_ctx, :])).astype(oc_ref.dtype)
    out_chunk = 512

    def emit(c, carry):
        r0 = pl.multiple_of(c * out_chunk, out_chunk)
        rec = h_s[0, pl.ds(off_l + r0, out_chunk), :] + h_s[1, pl.ds(off_l + r0, out_chunk), :]
        ol_ref[pl.ds(r0, out_chunk), :] = (jax.nn.gelu(gl_ref[pl.ds(r0, out_chunk), :]) * rec).astype(ol_ref.dtype)
        return carry

    lax.fori_loop(0, t_lat // out_chunk, emit, 0)


def _rglru(rg_l, rg_c, conv_w, conv_b, wa, ba, wx, bx, lam, batch, seq, ctx_len):
    nb = RG_BLOCKS
    bs = RG_BS
    t_all = seq + ctx_len
    t_pad = t_all + 2 * SUBLANES * SEG_PAD
    blk = lambda b, j: (0, j)
    return pl.pallas_call(
        functools.partial(_rg_kernel, t_lat=seq, t_ctx=ctx_len),
        grid=(batch, nb),
        in_specs=[pl.BlockSpec((seq, bs), lambda b, j: (b, j)),
                  pl.BlockSpec((ctx_len, bs), lambda b, j: (b, j)),
                  pl.BlockSpec((seq, bs), lambda b, j: (b, nb + j)),
                  pl.BlockSpec((ctx_len, bs), lambda b, j: (b, nb + j)),
                  pl.BlockSpec((RG_CONV, bs), blk),
                  pl.BlockSpec((1, bs), blk),
                  pl.BlockSpec((2, None, bs, bs), lambda b, j: (0, j, 0, 0)),
                  pl.BlockSpec((2, bs), blk),
                  pl.BlockSpec((2, None, bs, bs), lambda b, j: (0, j, 0, 0)),
                  pl.BlockSpec((2, bs), blk),
                  pl.BlockSpec((2, bs), blk)],
        out_specs=[pl.BlockSpec((seq, bs), lambda b, j: (b, j)),
                   pl.BlockSpec((ctx_len, bs), lambda b, j: (b, j))],
        out_shape=[jax.ShapeDtypeStruct((batch * seq, RG_WIDTH), BF16),
                   jax.ShapeDtypeStruct((batch * ctx_len, RG_WIDTH), BF16)],
        scratch_shapes=[pltpu.VMEM((seq + 2 * SUBLANES, bs), F32),
                        pltpu.VMEM((t_all, bs), F32),
                        pltpu.VMEM((2, t_pad, bs), F32),
                        pltpu.VMEM((2, t_pad, bs), F32),
                        pltpu.VMEM((2, t_pad, bs), F32)],
        compiler_params=_cparams("arbitrary", "arbitrary"),
        name="rglru",
    )(rg_l, rg_c, rg_l, rg_c, conv_w, conv_b.reshape(1, -1), wa, ba, wx, bx, lam)


def _merge_kernel(m_ref, r_ref, n_ref, gm_ref, gr_ref, gn_ref, wm_ref, wr_ref, wn_ref, o_ref):
    y = gm_ref[...].astype(F32) * jnp.dot(m_ref[...], wm_ref[...], preferred_element_type=F32)
    y = y + gr_ref[...].astype(F32) * jnp.dot(r_ref[...], wr_ref[...], preferred_element_type=F32)
    y = y + gn_ref[...].astype(F32) * jnp.dot(n_ref[...], wn_ref[...], preferred_element_type=F32)
    o_ref[...] = y.astype(o_ref.dtype)


def _merge(mla_o, rg_o, na_o, gates, wm, wr, wn):
    m, k = mla_o.shape
    n = wm.shape[1]
    tm = ROW_TILE
    tn = 1024
    nj = n // tn
    a_spec = pl.BlockSpec((tm, k), lambda j, i: (i, 0))
    w_spec = pl.BlockSpec((k, tn), lambda j, i: (0, j))
    return pl.pallas_call(
        _merge_kernel,
        grid=(nj, m // tm),
        in_specs=[a_spec, a_spec, a_spec,
                  pl.BlockSpec((tm, tn), lambda j, i: (i, j)),
                  pl.BlockSpec((tm, tn), lambda j, i: (i, nj + j)),
                  pl.BlockSpec((tm, tn), lambda j, i: (i, 2 * nj + j)),
                  w_spec, w_spec, w_spec],
        out_specs=pl.BlockSpec((tm, tn), lambda j, i: (i, j)),
        out_shape=jax.ShapeDtypeStruct((m, n), BF16),
        compiler_params=_cparams("arbitrary", "arbitrary"),
        name="merge",
    )(mla_o, rg_o, na_o, gates, gates, gates, wm, wr, wn)


def _proj_res_kernel(y_ref, w_ref, x_ref, g_ref, gate_ref, o_ref):
    z = jnp.dot(y_ref[...], w_ref[...], preferred_element_type=F32)
    o_ref[...] = x_ref[...] + gate_ref[...] * _rms(z, g_ref[...])


def _proj_residual(y, w, x, g, mods, k_gate, cond_row):
    m, d = x.shape
    tm = ROW_TILE
    return pl.pallas_call(
        _proj_res_kernel,
        grid=(m // tm,),
        in_specs=[pl.BlockSpec((tm, y.shape[1]), lambda i: (i, 0)),
                  pl.BlockSpec(w.shape, lambda i: (0, 0)),
                  pl.BlockSpec((tm, d), lambda i: (i, 0)),
                  pl.BlockSpec((1, d), lambda i: (0, 0)),
                  _mod_spec(k_gate, cond_row)],
        out_specs=pl.BlockSpec((tm, d), lambda i: (i, 0)),
        out_shape=jax.ShapeDtypeStruct((m, d), F32),
        compiler_params=_cparams("arbitrary"),
        name="out_proj",
    )(y, w, x, g.reshape(1, d), mods)


def _ffn_up_kernel(h_ref, wg_ref, wu_ref, o_ref):
    h = h_ref[...]
    g = jnp.dot(h, wg_ref[...], preferred_element_type=F32)
    u = jnp.dot(h, wu_ref[...], preferred_element_type=F32)
    o_ref[...] = (g * _sigmoid(g) * u).astype(o_ref.dtype)


def _ffn_up(h, wg, wu):
    m, k = h.shape
    n = wg.shape[1]
    tm = ROW_TILE
    tn = 512
    w_spec = pl.BlockSpec((k, tn), lambda j, i: (0, j))
    return pl.pallas_call(
        _ffn_up_kernel,
        grid=(n // tn, m // tm),
        in_specs=[pl.BlockSpec((tm, k), lambda j, i: (i, 0)), w_spec, w_spec],
        out_specs=pl.BlockSpec((tm, tn), lambda j, i: (i, j)),
        out_shape=jax.ShapeDtypeStruct((m, n), BF16),
        compiler_params=_cparams("arbitrary", "arbitrary"),
        name="ffn_up",
    )(h, wg, wu)


def _ffn_down_kernel(a_ref, w_ref, x_ref, g_ref, gate_ref, o_ref, acc_ref):
    kk = pl.program_id(1)

    @pl.when(kk == 0)
    def _():
        acc_ref[...] = jnp.zeros_like(acc_ref)

    acc_ref[...] += jnp.dot(a_ref[...], w_ref[...], preferred_element_type=F32)

    @pl.when(kk == pl.num_programs(1) - 1)
    def _():
        o_ref[...] = x_ref[...] + gate_ref[...] * _rms(acc_ref[...], g_ref[...])


def _ffn_down(act, w, x, g, mods, k_gate, cond_row):
    m, d = x.shape
    kdim = act.shape[1]
    tm = ROW_TILE
    tk = kdim // 4
    return pl.pallas_call(
        _ffn_down_kernel,
        grid=(m // tm, kdim // tk),
        in_specs=[pl.BlockSpec((tm, tk), lambda i, k: (i, k)),
                  pl.BlockSpec((tk, d), lambda i, k: (k, 0)),
                  pl.BlockSpec((tm, d), lambda i, k: (i, 0)),
                  pl.BlockSpec((1, d), lambda i, k: (0, 0)),
                  _mod_spec(k_gate, cond_row)],
        out_specs=pl.BlockSpec((tm, d), lambda i, k: (i, 0)),
        out_shape=jax.ShapeDtypeStruct((m, d), F32),
        scratch_shapes=[pltpu.VMEM((tm, d), F32)],
        compiler_params=_cparams("arbitrary", "arbitrary"),
        name="ffn_down",
    )(act, w, x, g.reshape(1, d), mods)


def _rope_tables(n_tok):
    t = jnp.arange(n_tok, dtype=jnp.int32)
    row = (t // GRID_W).astype(F32)
    col = (t % GRID_W).astype(F32)
    n_freq = MLA_ROPE // 4
    inv = ROPE_BASE ** (-jnp.arange(n_freq, dtype=F32) / n_freq)
    ang = jnp.stack([row[:, None] * inv, col[:, None] * inv], axis=1)
    cos = jnp.broadcast_to(jnp.cos(ang)[:, :, None, :], (n_tok, 2, 2, n_freq))
    sin = jnp.sin(ang)
    sin = jnp.stack([-sin, sin], axis=2)
    return cos.reshape(n_tok, MLA_ROPE), sin.reshape(n_tok, MLA_ROPE)


def _half_swap_perm():
    n_freq = MLA_ROPE // 4
    return np.arange(MLA_ROPE).reshape(2, 2, n_freq)[:, ::-1, :].reshape(MLA_ROPE)


def _layer_weights(w_in, w_q_b, w_kv_b):
    perm = _half_swap_perm()
    o = np.cumsum((0, MLA_Q_RANK, MLA_KV_RANK, MLA_ROPE, RG_WIDTH, RG_WIDTH, NA_WIDTH, NA_WIDTH, NA_WIDTH,
                   D_MODEL, D_MODEL, D_MODEL))
    w_a = jnp.concatenate([w_in[:, :o[3]], w_in[:, o[2]:o[3]][:, perm]], axis=1).astype(BF16)
    w_rg = w_in[:, o[3]:o[5]].astype(BF16)
    w_na_qk = jnp.concatenate([w_in[:, o[5]:o[6]] * (NA_SCALE * LOG2E), w_in[:, o[6]:o[7]]], axis=1).astype(BF16)
    w_na_vt = w_in[:, o[7]:o[8]].T.astype(BF16)
    w_gt = w_in[:, o[8]:o[11]].astype(BF16)
    wq = w_q_b.reshape(MLA_Q_RANK, MLA_HEADS, MLA_QK)
    wq_rope = wq[:, :, MLA_NOPE:]
    wq = jnp.concatenate([wq[:, :, :MLA_NOPE].reshape(MLA_Q_RANK, -1),
                          wq_rope.reshape(MLA_Q_RANK, -1),
                          wq_rope[:, :, perm].reshape(MLA_Q_RANK, -1)], axis=1)
    wq = (wq * (MLA_SCALE * LOG2E)).astype(BF16)
    wkv = w_kv_b.reshape(MLA_KV_RANK, MLA_HEADS, MLA_NOPE + MLA_V)
    wk = wkv[:, :, :MLA_NOPE].reshape(MLA_KV_RANK, -1).astype(BF16)
    wvt = wkv[:, :, MLA_NOPE:].reshape(MLA_KV_RANK, -1).T.astype(BF16)
    return w_a, w_rg, w_na_qk, w_na_vt, w_gt, wq, wk, wvt


def kernel(x, c, ctx, c_ctx, w_ada, b_ada, g_mix_pre, g_mix_post, g_ffn_pre, g_ffn_post, w_in, g_q_a, w_q_b, g_kv_a, w_kv_b, w_mla_o, conv_w, conv_b, rg_wa, rg_ba, rg_wx, rg_bx, rg_lambda, w_rg_o, na_rpb, w_na_o, w_out, w_ffn_gate, w_ffn_up, w_ffn_down):
    batch, seq, d = x.shape
    ctx_len = ctx.shape[1]
    depth = w_ada.shape[0]
    n_rows = seq // GRID_W
    assert d == D_MODEL and batch + 1 <= COND_ROWS
    assert seq % ROW_TILE == 0 and (batch * ctx_len) % ROW_TILE == 0 and seq % KEY_CHUNK == 0
    assert seq % GRID_W == 0 and n_rows % NA_ROWS_PER_STEP == 0 and n_rows >= 2 * NA_KEY_ROWS
    lat_row = lambda i: i // (seq // ROW_TILE)
    ctx_row = lambda i: batch
    hq, hv = MLA_HEADS, MLA_HEADS * MLA_V
    nh, hd = NA_HEADS, NA_HEAD_DIM

    cond = jnp.zeros((COND_ROWS, d), F32).at[:batch].set(c).at[batch].set(c_ctx)
    mods = _ada(cond, w_ada, b_ada).reshape(depth, COND_ROWS, 6, 1, d)

    cos_l, sin_l = _rope_tables(seq)
    cos_c = jnp.ones((batch * ctx_len, MLA_ROPE), F32)
    sin_c = jnp.zeros((batch * ctx_len, MLA_ROPE), F32)

    xl = x.reshape(batch * seq, d)
    xc = ctx.reshape(batch * ctx_len, d)
    nq = seq // ROW_TILE
    ctx_plan = [(0, 0, ctx_len)]
    lat_plan = ctx_plan + [(1, c0, KEY_CHUNK) for c0 in range(0, seq, KEY_CHUNK)]

    for l in range(depth):
        last = l == depth - 1
        md = mods[l]
        w_a, w_rg, w_na_qk, w_na_vt, w_gt, wq, wk, wvt = _layer_weights(w_in[l], w_q_b[l], w_kv_b[l])

        hl = _modulate(xl, g_mix_pre[l], md, 0, 1, lat_row)
        hc = _modulate(xc, g_mix_pre[l], md, 0, 1, ctx_row)
        a_l = _matmul(hl, w_a, F32, w_a.shape[1], name="proj_mla")
        a_c = _matmul(hc, w_a, F32, w_a.shape[1], name="proj_mla")
        rg_l = _matmul(hl, w_rg, F32, 1024, name="proj_rg")
        rg_c = _matmul(hc, w_rg, F32, 1024, name="proj_rg")
        nqk_l = _matmul(hl, w_na_qk, BF16, 1024, name="proj_na_qk")
        nqk_c = _matmul(hc, w_na_qk, BF16, 1024, name="proj_na_qk")
        nvt_l = _matmul_nt(w_na_vt, hl, BF16, 1024, name="proj_na_vt")
        nvt_c = _matmul_nt(w_na_vt, hc, BF16, 1024, name="proj_na_vt")
        gt_l = _matmul(hl, w_gt, BF16, 1024, act="sigmoid", name="proj_gate")

        q_l, k_l, vt_l = _mla_qkv(a_l, g_q_a[l], g_kv_a[l], wq, wk, wvt, cos_l, sin_l)
        q_c, k_c, vt_c = _mla_qkv(a_c, g_q_a[l], g_kv_a[l], wq, wk, wvt, cos_c, sin_c)
        mla_l = _attend(
            (batch, hq, nq), lat_plan, q_l,
            pl.BlockSpec((None, ROW_TILE, MLA_QK), lambda b, h, i: (h, b * nq + i, 0)),
            [k_c, k_l],
            [pl.BlockSpec((None, ctx_len, MLA_QK), lambda b, h, i: (h, b, 0)),
             pl.BlockSpec((None, seq, MLA_QK), lambda b, h, i: (h, b, 0))],
            [vt_c, vt_l],
            [pl.BlockSpec((MLA_V, ctx_len), lambda b, h, i: (h, b)),
             pl.BlockSpec((MLA_V, seq), lambda b, h, i: (h, b))],
            jax.ShapeDtypeStruct((batch * seq, hv), BF16),
            pl.BlockSpec((ROW_TILE, MLA_V), lambda b, h, i: (b * nq + i, h)),
            "mla_attn")

        rgo_l, rgo_c = _rglru(rg_l, rg_c, conv_w[l], conv_b[l], rg_wa[l].astype(BF16), rg_ba[l],
                              rg_wx[l].astype(BF16), rg_bx[l], rg_lambda[l], batch, seq, ctx_len)

        nao_l = _natten(nqk_l, nvt_l, nqk_c, nvt_c, _natten_rpb_tiles(na_rpb[l]), batch, seq, ctx_len)

        wm, wr, wn, wo = (w_mla_o[l].astype(BF16), w_rg_o[l].astype(BF16), w_na_o[l].astype(BF16),
                          w_out[l].astype(BF16))
        wfg, wfu, wfd = w_ffn_gate[l].astype(BF16), w_ffn_up[l].astype(BF16), w_ffn_down[l].astype(BF16)

        def tail(xs, mla_o, rg_o, na_o, gates, cond_row):
            y = _merge(mla_o, rg_o, na_o, gates, wm, wr, wn)
            xs = _proj_residual(y, wo, xs, g_mix_post[l], md, 2, cond_row)
            h2 = _modulate(xs, g_ffn_pre[l], md, 3, 4, cond_row)
            act = _ffn_up(h2, wfg, wfu)
            return _ffn_down(act, wfd, xs, g_ffn_post[l], md, 5, cond_row)

        xl = tail(xl, mla_l, rgo_l, nao_l, gt_l, lat_row)

        if not last:
            gt_c = _matmul(hc, w_gt, BF16, 1024, act="sigmoid", name="proj_gate")
            mla_c = _attend(
                (batch, hq), ctx_plan, q_c,
                pl.BlockSpec((None, ctx_len, MLA_QK), lambda b, h: (h, b, 0)),
                [k_c], [pl.BlockSpec((None, ctx_len, MLA_QK), lambda b, h: (h, b, 0))],
                [vt_c], [pl.BlockSpec((MLA_V, ctx_len), lambda b, h: (h, b))],
                jax.ShapeDtypeStruct((batch * ctx_len, hv), BF16),
                pl.BlockSpec((ctx_len, MLA_V), lambda b, h: (b, h)),
                "mla_attn_ctx")
            nao_c = _attend(
                (batch, nh), ctx_plan, nqk_c,
                pl.BlockSpec((ctx_len, hd), lambda b, h: (b, h)),
                [nqk_c], [pl.BlockSpec((ctx_len, hd), lambda b, h: (b, nh + h))],
                [nvt_c], [pl.BlockSpec((hd, ctx_len), lambda b, h: (h, b))],
                jax.ShapeDtypeStruct((batch * ctx_len, NA_WIDTH), BF16),
                pl.BlockSpec((ctx_len, hd), lambda b, h: (b, h)),
                "na_attn_ctx")
            xc = tail(xc, mla_c, rgo_c, nao_c, gt_c, ctx_row)

    return xl.reshape(batch, seq, d)
```

```python
import functools
import math

import numpy as np
import jax
import jax.numpy as jnp
from jax import lax
from jax.experimental import pallas as pl
from jax.experimental.pallas import tpu as pltpu

F32 = jnp.float32
BF16 = jnp.bfloat16

D_MODEL = 2048
GRID_W = 64
MLA_HEADS = 8
MLA_Q_RANK = 512
MLA_KV_RANK = 256
MLA_NOPE = 128
MLA_ROPE = 64
MLA_V = 128
MLA_QK = MLA_NOPE + MLA_ROPE
MLA_SCALE = 1.0 / math.sqrt(MLA_NOPE + MLA_ROPE)
ROPE_BASE = 10000.0
RG_WIDTH = 1024
RG_BLOCKS = 8
RG_BS = RG_WIDTH // RG_BLOCKS
RG_CONV = 4
RG_C = 8.0
NA_HEADS = 8
NA_HEAD_DIM = 128
NA_WIDTH = NA_HEADS * NA_HEAD_DIM
NA_WIN_ROWS = 8
NA_WIN_COLS = 16
NA_SCALE = 1.0 / math.sqrt(NA_HEAD_DIM)
NORM_EPS = 1e-6
NEG_INF = -1e30
LOG2E = math.log2(math.e)

SUBLANES = 8
ROW_TILE = 512
KEY_CHUNK = 512
COND_ROWS = 8
NA_ROWS_PER_STEP = 8
NA_KEY_ROWS = 16
NA_KEY_BLOCK = 4
SEG_PAD = 4
VMEM_LIMIT = 48 * 1024 * 1024

NT_DIMS = (((1,), (1,)), ((), ()))


def _cparams(*sem):
    return pltpu.CompilerParams(dimension_semantics=sem, vmem_limit_bytes=VMEM_LIMIT)


def _rms(x, g):
    return x * lax.rsqrt(jnp.mean(x * x, axis=-1, keepdims=True) + NORM_EPS) * g


def _sigmoid(x):
    return 0.5 * jnp.tanh(0.5 * x) + 0.5


def _ada_kernel(c_ref, w_ref, b_ref, o_ref):
    c = c_ref[...]
    s = (c * jax.nn.sigmoid(c)).astype(BF16)
    o_ref[...] = jnp.dot(s, w_ref[...].astype(BF16), preferred_element_type=F32) + b_ref[...]


def _ada(cond, w_ada, b_ada):
    depth, d, n = w_ada.shape
    tn = 1024
    return pl.pallas_call(
        _ada_kernel,
        grid=(depth, n // tn),
        in_specs=[pl.BlockSpec((COND_ROWS, d), lambda l, j: (0, 0)),
                  pl.BlockSpec((None, d, tn), lambda l, j: (l, 0, j)),
                  pl.BlockSpec((None, 1, tn), lambda l, j: (l, 0, j))],
        out_specs=pl.BlockSpec((None, COND_ROWS, tn), lambda l, j: (l, 0, j)),
        out_shape=jax.ShapeDtypeStruct((depth, COND_ROWS, n), F32),
        compiler_params=_cparams("arbitrary", "arbitrary"),
        name="ada",
    )(cond, w_ada, b_ada.reshape(depth, 1, n))


def _mod_spec(chunk, cond_row):
    return pl.BlockSpec((None, None, 1, D_MODEL), lambda i, *_: (cond_row(i), chunk, 0, 0))


def _modulate_kernel(x_ref, g_ref, sh_ref, sc_ref, o_ref):
    y = _rms(x_ref[...], g_ref[...])
    o_ref[...] = (y * (1.0 + sc_ref[...]) + sh_ref[...]).astype(o_ref.dtype)


def _modulate(x, g, mods, k_shift, k_scale, cond_row):
    m, d = x.shape
    tm = ROW_TILE
    return pl.pallas_call(
        _modulate_kernel,
        grid=(m // tm,),
        in_specs=[pl.BlockSpec((tm, d), lambda i: (i, 0)),
                  pl.BlockSpec((1, d), lambda i: (0, 0)),
                  _mod_spec(k_shift, cond_row),
                  _mod_spec(k_scale, cond_row)],
        out_specs=pl.BlockSpec((tm, d), lambda i: (i, 0)),
        out_shape=jax.ShapeDtypeStruct((m, d), BF16),
        compiler_params=_cparams("arbitrary"),
        name="modulate",
    )(x, g.reshape(1, d), mods, mods)


def _mm_kernel(a_ref, b_ref, o_ref, *, act):
    acc = jnp.dot(a_ref[...], b_ref[...], preferred_element_type=F32)
    if act == "sigmoid":
        acc = _sigmoid(acc)
    o_ref[...] = acc.astype(o_ref.dtype)


def _matmul(a, b, out_dtype, tn, act=None, name="matmul"):
    m, k = a.shape
    n = b.shape[1]
    tm = ROW_TILE
    return pl.pallas_call(
        functools.partial(_mm_kernel, act=act),
        grid=(n // tn, m // tm),
        in_specs=[pl.BlockSpec((tm, k), lambda j, i: (i, 0)),
                  pl.BlockSpec((k, tn), lambda j, i: (0, j))],
        out_specs=pl.BlockSpec((tm, tn), lambda j, i: (i, j)),
        out_shape=jax.ShapeDtypeStruct((m, n), out_dtype),
        compiler_params=_cparams("arbitrary", "arbitrary"),
        name=name,
    )(a, b)


def _mm_nt_kernel(w_ref, x_ref, o_ref):
    o_ref[...] = lax.dot_general(w_ref[...], x_ref[...], NT_DIMS, preferred_element_type=F32).astype(o_ref.dtype)


def _matmul_nt(wt, x, out_dtype, tn, name):
    n, k = wt.shape
    m = x.shape[0]
    tm = ROW_TILE
    return pl.pallas_call(
        _mm_nt_kernel,
        grid=(n // tn, m // tm),
        in_specs=[pl.BlockSpec((tn, k), lambda j, i: (j, 0)),
                  pl.BlockSpec((tm, k), lambda j, i: (i, 0))],
        out_specs=pl.BlockSpec((tn, tm), lambda j, i: (j, i)),
        out_shape=jax.ShapeDtypeStruct((n, m), out_dtype),
        compiler_params=_cparams("arbitrary", "arbitrary"),
        name=name,
    )(wt, x)


def _mla_qkv_kernel(a_ref, gq_ref, gkv_ref, wq_ref, wk_ref, wvt_ref, c_ref, s_ref, q_ref, k_ref, vt_ref):
    a = a_ref[...]
    qa = _rms(a[:, :MLA_Q_RANK], gq_ref[...]).astype(BF16)
    kva = _rms(a[:, MLA_Q_RANK:MLA_Q_RANK + MLA_KV_RANK], gkv_ref[...]).astype(BF16)
    q = jnp.dot(qa, wq_ref[...], preferred_element_type=F32)
    kn = jnp.dot(kva, wk_ref[...], preferred_element_type=F32)
    vt_ref[...] = lax.dot_general(wvt_ref[...], kva, NT_DIMS, preferred_element_type=F32).astype(BF16)
    cos = c_ref[...]
    sin = s_ref[...]
    o_kr = MLA_Q_RANK + MLA_KV_RANK
    kr = (a[:, o_kr:o_kr + MLA_ROPE] * cos + a[:, o_kr + MLA_ROPE:o_kr + 2 * MLA_ROPE] * sin).astype(BF16)
    n_nope = MLA_HEADS * MLA_NOPE
    n_rope = MLA_HEADS * MLA_ROPE
    for h in range(MLA_HEADS):
        q_ref[h, :, 0:MLA_NOPE] = q[:, h * MLA_NOPE:(h + 1) * MLA_NOPE].astype(BF16)
        r0 = n_nope + h * MLA_ROPE
        qr = q[:, r0:r0 + MLA_ROPE] * cos + q[:, r0 + n_rope:r0 + n_rope + MLA_ROPE] * sin
        q_ref[h, :, MLA_NOPE:MLA_QK] = qr.astype(BF16)
        k_ref[h, :, 0:MLA_NOPE] = kn[:, h * MLA_NOPE:(h + 1) * MLA_NOPE].astype(BF16)
        k_ref[h, :, MLA_NOPE:MLA_QK] = kr


def _mla_qkv(a, gq, gkv, wq, wk, wvt, cos, sin):
    m, wa = a.shape
    tm = ROW_TILE
    nt = cos.shape[0] // tm
    hv = MLA_HEADS * MLA_V
    return pl.pallas_call(
        _mla_qkv_kernel,
        grid=(m // tm,),
        in_specs=[pl.BlockSpec((tm, wa), lambda i: (i, 0)),
                  pl.BlockSpec((1, MLA_Q_RANK), lambda i: (0, 0)),
                  pl.BlockSpec((1, MLA_KV_RANK), lambda i: (0, 0)),
                  pl.BlockSpec(wq.shape, lambda i: (0, 0)),
                  pl.BlockSpec(wk.shape, lambda i: (0, 0)),
                  pl.BlockSpec(wvt.shape, lambda i: (0, 0)),
                  pl.BlockSpec((tm, MLA_ROPE), lambda i: (i % nt, 0)),
                  pl.BlockSpec((tm, MLA_ROPE), lambda i: (i % nt, 0))],
        out_specs=[pl.BlockSpec((MLA_HEADS, tm, MLA_QK), lambda i: (0, i, 0)),
                   pl.BlockSpec((MLA_HEADS, tm, MLA_QK), lambda i: (0, i, 0)),
                   pl.BlockSpec((hv, tm), lambda i: (0, i))],
        out_shape=[jax.ShapeDtypeStruct((MLA_HEADS, m, MLA_QK), BF16),
                   jax.ShapeDtypeStruct((MLA_HEADS, m, MLA_QK), BF16),
                   jax.ShapeDtypeStruct((hv, m), BF16)],
        compiler_params=_cparams("arbitrary"),
        name="mla_qkv",
    )(a, gq.reshape(1, -1), gkv.reshape(1, -1), wq, wk, wvt, cos, sin)


def _flash_t(q, chunks):
    scores = []
    m = None
    for k, _, bias_t in chunks:
        s = lax.dot_general(k, q, NT_DIMS, preferred_element_type=F32)
        if bias_t is not None:
            s = s + bias_t
        scores.append(s)
        mc = s.max(axis=0, keepdims=True)
        m = mc if m is None else jnp.maximum(m, mc)
    den = acc = None
    for s, (_, vt, _) in zip(scores, chunks):
        p = jnp.exp2(s - m)
        ps = p.sum(axis=0, keepdims=True)
        pv = jnp.dot(vt, p.astype(BF16), preferred_element_type=F32)
        den = ps if den is None else den + ps
        acc = pv if acc is None else acc + pv
    return acc / den


def _attn_pipeline(n_tiles, load_q, chunks_of, store_out, s_bufs, m_bufs):
    def scores(q, chunk, s_buf, off, m):
        n, load_k, _, load_bias = chunk
        s = lax.dot_general(load_k(), q, NT_DIMS, preferred_element_type=F32)
        if load_bias is not None:
            s = s + load_bias()
        s_buf[off:off + n, :] = s
        mc = s.max(axis=0, keepdims=True)
        return mc if m is None else jnp.maximum(m, mc)

    def values(chunk, s_buf, off, m, den, acc):
        n, _, load_vt, _ = chunk
        p = jnp.exp2(s_buf[off:off + n, :] - m)
        ps = p.sum(axis=0, keepdims=True)
        pv = jnp.dot(load_vt(), p.astype(BF16), preferred_element_type=F32)
        return (ps if den is None else den + ps), (pv if acc is None else acc + pv)

    q0 = load_q(0)
    m0, off = None, 0
    for chunk in chunks_of(0):
        m0 = scores(q0, chunk, s_bufs[0], off, m0)
        off += chunk[0]
    m_bufs[0][...] = m0

    def step(t, cur, nxt):
        t_next = jnp.minimum(t + 1, n_tiles - 1)
        q_next = load_q(t_next)
        m = m_bufs[cur][...]
        m_next = den = acc = None
        off = 0
        for c_cur, c_next in zip(chunks_of(t), chunks_of(t_next)):
            m_next = scores(q_next, c_next, s_bufs[nxt], off, m_next)
            den, acc = values(c_cur, s_bufs[cur], off, m, den, acc)
            off += c_cur[0]
        m_bufs[nxt][...] = m_next
        store_out(t, acc / den)

    def two_steps(i, carry):
        step(2 * i, 0, 1)
        step(2 * i + 1, 1, 0)
        return carry

    assert n_tiles % 2 == 0
    lax.fori_loop(0, n_tiles // 2, two_steps, 0)


def _mla_attn_kernel(q_ref, kc_ref, kl_ref, vtc_ref, vtl_ref, o_ref, s_a, s_b, m_a, m_b, *, tq, chunk):
    seq = kl_ref.shape[0]
    chunks = [(kc_ref.shape[0], lambda: kc_ref[...], lambda: vtc_ref[...], None)]
    for c0 in range(0, seq, chunk):
        chunks.append((chunk, lambda c0=c0: kl_ref[c0:c0 + chunk, :], lambda c0=c0: vtl_ref[:, c0:c0 + chunk], None))

    def load_q(t):
        return q_ref[pl.ds(pl.multiple_of(t * tq, tq), tq), :]

    def store_out(t, out_t):
        o_ref[pl.ds(pl.multiple_of(t * tq, tq), tq), :] = out_t.T.astype(o_ref.dtype)

    _attn_pipeline(seq // tq, load_q, lambda t: chunks, store_out, (s_a, s_b), (m_a, m_b))


def _mla_attn(q_l, k_c, k_l, vt_c, vt_l, batch, seq, ctx_len):
    tq = ROW_TILE
    n_keys = seq + ctx_len
    return pl.pallas_call(
        functools.partial(_mla_attn_kernel, tq=tq, chunk=KEY_CHUNK),
        grid=(batch, MLA_HEADS),
        in_specs=[pl.BlockSpec((None, seq, MLA_QK), lambda b, h: (h, b, 0)),
                  pl.BlockSpec((None, ctx_len, MLA_QK), lambda b, h: (h, b, 0)),
                  pl.BlockSpec((None, seq, MLA_QK), lambda b, h: (h, b, 0)),
                  pl.BlockSpec((MLA_V, ctx_len), lambda b, h: (h, b)),
                  pl.BlockSpec((MLA_V, seq), lambda b, h: (h, b))],
        out_specs=pl.BlockSpec((seq, MLA_V), lambda b, h: (b, h)),
        out_shape=jax.ShapeDtypeStruct((batch * seq, MLA_HEADS * MLA_V), BF16),
        scratch_shapes=[pltpu.VMEM((n_keys, tq), F32), pltpu.VMEM((n_keys, tq), F32),
                        pltpu.VMEM((1, tq), F32), pltpu.VMEM((1, tq), F32)],
        compiler_params=_cparams("arbitrary", "arbitrary"),
        name="mla_attn",
    )(q_l, k_c, k_l, vt_c, vt_l)


def _attn_kernel(*refs, plan, nseg):
    q_ref = refs[0]
    k_refs = refs[1:1 + nseg]
    vt_refs = refs[1 + nseg:1 + 2 * nseg]
    o_ref = refs[1 + 2 * nseg]
    chunks = [(k_refs[s][c0:c0 + n, :], vt_refs[s][:, c0:c0 + n], None) for s, c0, n in plan]
    o_ref[...] = _flash_t(q_ref[...], chunks).T.astype(o_ref.dtype)


def _attend(grid, plan, q, q_spec, ks, k_specs, vts, vt_specs, out_shape, out_spec, name):
    nseg = len(ks)
    return pl.pallas_call(
        functools.partial(_attn_kernel, plan=tuple(plan), nseg=nseg),
        grid=grid,
        in_specs=[q_spec] + list(k_specs) + list(vt_specs),
        out_specs=out_spec,
        out_shape=out_shape,
        compiler_params=_cparams(*(["arbitrary"] * len(grid))),
        name=name,
    )(q, *ks, *vts)


def _natten_row_structure(n_rows):
    def block(r0):
        ws = np.clip(r0 - NA_WIN_ROWS // 2, 0, n_rows - NA_KEY_ROWS)
        r = r0 + np.arange(NA_ROWS_PER_STEP)
        start = np.clip(r - NA_WIN_ROWS // 2, 0, n_rows - NA_WIN_ROWS)
        key_row = ws + np.arange(NA_KEY_ROWS)
        kr = key_row[None, :] - start[:, None]
        valid = (kr >= 0) & (kr < NA_WIN_ROWS)
        dr = key_row[None, :] - r[:, None] + (NA_WIN_ROWS - 1)
        return valid, np.where(valid, dr, 0)

    blocks = [block(r0) for r0 in range(0, n_rows, NA_ROWS_PER_STEP)]
    for valid, dr in blocks[2:-1]:
        assert (valid == blocks[1][0]).all() and (dr == blocks[1][1]).all()
    return blocks[0], blocks[1], blocks[-1]


def _natten_kernel(q_ref, k_ref, vt_ref, kc_ref, vtc_ref, rpb_ref, o_ref, bias, s_a, s_b, m_a, m_b, *, structure):
    @pl.when(pl.program_id(1) == 0)
    def _():
        masked = jnp.full((GRID_W, GRID_W), NEG_INF, F32)
        for v, (valid, dr) in enumerate(structure):
            for i in range(NA_KEY_ROWS):
                tiles = [rpb_ref[int(dr[j, i])] if valid[j, i] else masked for j in range(NA_ROWS_PER_STEP)]
                bias[v, i * GRID_W:(i + 1) * GRID_W, :] = jnp.concatenate(tiles, axis=1)

    seq = k_ref.shape[0]
    n_rows = seq // GRID_W
    tq = NA_ROWS_PER_STEP * GRID_W
    n_tiles = seq // tq
    half = NA_KEY_ROWS * GRID_W // 2
    align = NA_KEY_BLOCK * GRID_W

    def chunks_of(t):
        variant = jnp.where(t == 0, 0, jnp.where(t == n_tiles - 1, 2, 1))
        first_row = jnp.clip(t * NA_ROWS_PER_STEP - NA_WIN_ROWS // 2, 0, n_rows - NA_KEY_ROWS)
        chunks = [(kc_ref.shape[0], lambda: kc_ref[...], lambda: vtc_ref[...], None)]
        for c in range(2):
            k0 = pl.multiple_of(first_row * GRID_W + c * half, align)
            chunks.append((half,
                           lambda k0=k0: k_ref[pl.ds(k0, half), :],
                           lambda k0=k0: vt_ref[:, pl.ds(k0, half)],
                           lambda c=c: bias[variant, c * half:(c + 1) * half, :]))
        return chunks

    def load_q(t):
        return q_ref[pl.ds(pl.multiple_of(t * tq, tq), tq), :]

    def store_out(t, out_t):
        o_ref[pl.ds(pl.multiple_of(t * tq, tq), tq), :] = out_t.T.astype(o_ref.dtype)

    _attn_pipeline(n_tiles, load_q, chunks_of, store_out, (s_a, s_b), (m_a, m_b))


def _natten(qk_l, vt_l, qk_c, vt_c, rpb_tiles, batch, seq, ctx_len):
    n_rows = seq // GRID_W
    tq = NA_ROWS_PER_STEP * GRID_W
    n_keys = NA_KEY_ROWS * GRID_W + ctx_len
    hd = NA_HEAD_DIM
    nh = NA_HEADS
    return pl.pallas_call(
        functools.partial(_natten_kernel, structure=_natten_row_structure(n_rows)),
        grid=(nh, batch),
        in_specs=[pl.BlockSpec((seq, hd), lambda h, b: (b, h)),
                  pl.BlockSpec((seq, hd), lambda h, b: (b, nh + h)),
                  pl.BlockSpec((hd, seq), lambda h, b: (h, b)),
                  pl.BlockSpec((ctx_len, hd), lambda h, b: (b, nh + h)),
                  pl.BlockSpec((hd, ctx_len), lambda h, b: (h, b)),
                  pl.BlockSpec((None,) + rpb_tiles.shape[1:], lambda h, b: (h, 0, 0, 0))],
        out_specs=pl.BlockSpec((seq, hd), lambda h, b: (b, h)),
        out_shape=jax.ShapeDtypeStruct((batch * seq, NA_WIDTH), BF16),
        scratch_shapes=[pltpu.VMEM((3, NA_KEY_ROWS * GRID_W, tq), F32),
                        pltpu.VMEM((n_keys, tq), F32), pltpu.VMEM((n_keys, tq), F32),
                        pltpu.VMEM((1, tq), F32), pltpu.VMEM((1, tq), F32)],
        compiler_params=_cparams("arbitrary", "arbitrary"),
        name="natten",
    )(qk_l, qk_l, vt_l, qk_c, vt_c, rpb_tiles)


def _natten_rpb_tiles(rpb):
    col = np.arange(GRID_W)
    c_start = np.clip(col - NA_WIN_COLS // 2, 0, GRID_W - NA_WIN_COLS)
    col_in = (col[None, :] >= c_start[:, None]) & (col[None, :] < c_start[:, None] + NA_WIN_COLS)
    dc = np.clip(col[None, :] - col[:, None] + (NA_WIN_COLS - 1), 0, 2 * NA_WIN_COLS - 2)
    onehot = (dc.T[None] == np.arange(2 * NA_WIN_COLS - 1)[:, None, None]).astype(np.float32)
    t = jnp.einsum("hdc,ckq->hdkq", rpb.astype(F32), onehot, precision=lax.Precision.HIGHEST)
    return jnp.where(col_in.T[None, None], t * LOG2E, NEG_INF)


def _softplus(x):
    return jnp.maximum(x, 0.0) + jnp.log1p(jnp.exp(-jnp.abs(x)))


def _rg_kernel(ul_ref, uc_ref, gl_ref, gc_ref, cw_ref, cb_ref, wa_ref, ba_ref, wx_ref, bx_ref, lam_ref,
               ol_ref, oc_ref, upad, ucv, a_s, b_s, h_s, *, t_lat, t_ctx):
    pad = SUBLANES
    cw = cw_ref[...]
    cb = cb_ref[...]
    zeros_pad = jnp.zeros((pad, RG_BS), F32)
    seg_c = t_ctx // SUBLANES + SEG_PAD
    seg_l = t_lat // SUBLANES + SEG_PAD
    off_l = SUBLANES * seg_c

    def conv_stream(u_ref, t_len, dst, chunk):
        upad[0:pad, :] = zeros_pad
        upad[pad:pad + t_len, :] = u_ref[...]
        upad[pad + t_len:2 * pad + t_len, :] = zeros_pad
        for c0 in range(0, t_len, chunk):
            y = cb
            for k in range(RG_CONV):
                y = y + cw[k:k + 1, :] * upad[pad + c0 + k - RG_CONV // 2:pad + c0 + k - RG_CONV // 2 + chunk, :]
            ucv[dst + c0:dst + c0 + chunk, :] = y

    conv_stream(uc_ref, t_ctx, 0, t_ctx)
    conv_stream(ul_ref, t_lat, t_ctx, 512)

    rate = (-RG_C * LOG2E) * _softplus(-lam_ref[...])
    coef_chunk = 256

    def coef_range(src0, dst0, n_chunks):
        def coef(c, carry):
            r0 = pl.multiple_of(src0 + c * coef_chunk, SUBLANES)
            w0 = pl.multiple_of(dst0 + c * coef_chunk, SUBLANES)
            u = ucv[pl.ds(r0, coef_chunk), :]
            ub = u.astype(BF16)
            for d in range(2):
                r = _sigmoid(jnp.dot(ub, wa_ref[d], preferred_element_type=F32) + ba_ref[d:d + 1, :])
                i = _sigmoid(jnp.dot(ub, wx_ref[d], preferred_element_type=F32) + bx_ref[d:d + 1, :])
                a = jnp.exp2(r * rate[d:d + 1, :])
                a_s[d, pl.ds(w0, coef_chunk), :] = a
                b_s[d, pl.ds(w0, coef_chunk), :] = jnp.sqrt((1.0 - a) * (1.0 + a)) * (i * u)
            return carry
        lax.fori_loop(0, n_chunks, coef, 0)

    coef_range(0, 0, t_ctx // coef_chunk)
    coef_range(t_ctx, off_l, t_lat // coef_chunk)
    for d in range(2):
        for lo, hi in ((t_ctx, off_l), (off_l + t_lat, off_l + SUBLANES * seg_l)):
            a_s[d, lo:hi, :] = jnp.ones((hi - lo, RG_BS), F32)
            b_s[d, lo:hi, :] = jnp.zeros((hi - lo, RG_BS), F32)

    sub = lax.broadcasted_iota(jnp.int32, (SUBLANES, RG_BS), 0)

    def scan_stream(off, seg, h0_f, h0_b):
        def ld(ref, d, j):
            return ref[d, pl.ds(off + j, SUBLANES, stride=seg), :]

        def sweep1(j, st):
            hf, pf, hb, pb = st
            jb = seg - 1 - j
            af = ld(a_s, 0, j)
            ab = ld(a_s, 1, jb)
            return (af * hf + ld(b_s, 0, j), af * pf, ab * hb + ld(b_s, 1, jb), ab * pb)

        z = jnp.zeros((SUBLANES, RG_BS), F32)
        o = jnp.ones((SUBLANES, RG_BS), F32)
        hf, pf, hb, pb = lax.fori_loop(0, seg, sweep1, (z, o, z, o), unroll=4)

        start_f = z
        c = h0_f
        for s in range(SUBLANES):
            start_f = jnp.where(sub == s, c, start_f)
            c = pf[s:s + 1, :] * c + hf[s:s + 1, :]
        end_f = c
        start_b = z
        c = h0_b
        for s in range(SUBLANES - 1, -1, -1):
            start_b = jnp.where(sub == s, c, start_b)
            c = pb[s:s + 1, :] * c + hb[s:s + 1, :]
        end_b = c

        def sweep2(j, st):
            hf, hb = st
            jb = seg - 1 - j
            hf = ld(a_s, 0, j) * hf + ld(b_s, 0, j)
            hb = ld(a_s, 1, jb) * hb + ld(b_s, 1, jb)
            h_s[0, pl.ds(off + j, SUBLANES, stride=seg), :] = hf
            h_s[1, pl.ds(off + jb, SUBLANES, stride=seg), :] = hb
            return (hf, hb)

        lax.fori_loop(0, seg, sweep2, (start_f, start_b), unroll=4)
        return end_f, end_b

    zero_row = jnp.zeros((1, RG_BS), F32)
    end_f, end_b = scan_stream(0, seg_c, zero_row, zero_row)
    scan_stream(off_l, seg_l, end_f, end_b)

    oc_ref[...] = (jax.nn.gelu(gc_ref[...]) * (h_s[0, 0:t_ctx, :] + h_s[1, 0:t_ctx, :])).astype(oc_ref.dtype)
    out_chunk = 512

    def emit(c, carry):
        r0 = pl.multiple_of(c * out_chunk, out_chunk)
        rec = h_s[0, pl.ds(off_l + r0, out_chunk), :] + h_s[1, pl.ds(off_l + r0, out_chunk), :]
        ol_ref[pl.ds(r0, out_chunk), :] = (jax.nn.gelu(gl_ref[pl.ds(r0, out_chunk), :]) * rec).astype(ol_ref.dtype)
        return carry

    lax.fori_loop(0, t_lat // out_chunk, emit, 0)


def _rglru(rg_l, rg_c, conv_w, conv_b, wa, ba, wx, bx, lam, batch, seq, ctx_len):
    nb = RG_BLOCKS
    bs = RG_BS
    t_all = seq + ctx_len
    t_pad = t_all + 2 * SUBLANES * SEG_PAD
    blk = lambda b, j: (0, j)
    return pl.pallas_call(
        functools.partial(_rg_kernel, t_lat=seq, t_ctx=ctx_len),
        grid=(batch, nb),
        in_specs=[pl.BlockSpec((seq, bs), lambda b, j: (b, j)),
                  pl.BlockSpec((ctx_len, bs), lambda b, j: (b, j)),
                  pl.BlockSpec((seq, bs), lambda b, j: (b, nb + j)),
                  pl.BlockSpec((ctx_len, bs), lambda b, j: (b, nb + j)),
                  pl.BlockSpec((RG_CONV, bs), blk),
                  pl.BlockSpec((1, bs), blk),
                  pl.BlockSpec((2, None, bs, bs), lambda b, j: (0, j, 0, 0)),
                  pl.BlockSpec((2, bs), blk),
                  pl.BlockSpec((2, None, bs, bs), lambda b, j: (0, j, 0, 0)),
                  pl.BlockSpec((2, bs), blk),
                  pl.BlockSpec((2, bs), blk)],
        out_specs=[pl.BlockSpec((seq, bs), lambda b, j: (b, j)),
                   pl.BlockSpec((ctx_len, bs), lambda b, j: (b, j))],
        out_shape=[jax.ShapeDtypeStruct((batch * seq, RG_WIDTH), BF16),
                   jax.ShapeDtypeStruct((batch * ctx_len, RG_WIDTH), BF16)],
        scratch_shapes=[pltpu.VMEM((seq + 2 * SUBLANES, bs), F32),
                        pltpu.VMEM((t_all, bs), F32),
                        pltpu.VMEM((2, t_pad, bs), F32),
                        pltpu.VMEM((2, t_pad, bs), F32),
                        pltpu.VMEM((2, t_pad, bs), F32)],
        compiler_params=_cparams("arbitrary", "arbitrary"),
        name="rglru",
    )(rg_l, rg_c, rg_l, rg_c, conv_w, conv_b.reshape(1, -1), wa, ba, wx, bx, lam)


def _merge_kernel(m_ref, r_ref, n_ref, gm_ref, gr_ref, gn_ref, wm_ref, wr_ref, wn_ref, o_ref):
    y = gm_ref[...].astype(F32) * jnp.dot(m_ref[...], wm_ref[...], preferred_element_type=F32)
    y = y + gr_ref[...].astype(F32) * jnp.dot(r_ref[...], wr_ref[...], preferred_element_type=F32)
    y = y + gn_ref[...].astype(F32) * jnp.dot(n_ref[...], wn_ref[...], preferred_element_type=F32)
    o_ref[...] = y.astype(o_ref.dtype)


def _merge(mla_o, rg_o, na_o, gates, wm, wr, wn):
    m, k = mla_o.shape
    n = wm.shape[1]
    tm = ROW_TILE
    tn = 1024
    nj = n // tn
    a_spec = pl.BlockSpec((tm, k), lambda j, i: (i, 0))
    w_spec = pl.BlockSpec((k, tn), lambda j, i: (0, j))
    return pl.pallas_call(
        _merge_kernel,
        grid=(nj, m // tm),
        in_specs=[a_spec, a_spec, a_spec,
                  pl.BlockSpec((tm, tn), lambda j, i: (i, j)),
                  pl.BlockSpec((tm, tn), lambda j, i: (i, nj + j)),
                  pl.BlockSpec((tm, tn), lambda j, i: (i, 2 * nj + j)),
                  w_spec, w_spec, w_spec],
        out_specs=pl.BlockSpec((tm, tn), lambda j, i: (i, j)),
        out_shape=jax.ShapeDtypeStruct((m, n), BF16),
        compiler_params=_cparams("arbitrary", "arbitrary"),
        name="merge",
    )(mla_o, rg_o, na_o, gates, gates, gates, wm, wr, wn)


def _proj_res_kernel(y_ref, w_ref, x_ref, g_ref, gate_ref, o_ref):
    z = jnp.dot(y_ref[...], w_ref[...], preferred_element_type=F32)
    o_ref[...] = x_ref[...] + gate_ref[...] * _rms(z, g_ref[...])


def _proj_residual(y, w, x, g, mods, k_gate, cond_row):
    m, d = x.shape
    tm = ROW_TILE
    return pl.pallas_call(
        _proj_res_kernel,
        grid=(m // tm,),
        in_specs=[pl.BlockSpec((tm, y.shape[1]), lambda i: (i, 0)),
                  pl.BlockSpec(w.shape, lambda i: (0, 0)),
                  pl.BlockSpec((tm, d), lambda i: (i, 0)),
                  pl.BlockSpec((1, d), lambda i: (0, 0)),
                  _mod_spec(k_gate, cond_row)],
        out_specs=pl.BlockSpec((tm, d), lambda i: (i, 0)),
        out_shape=jax.ShapeDtypeStruct((m, d), F32),
        compiler_params=_cparams("arbitrary"),
        name="out_proj",
    )(y, w, x, g.reshape(1, d), mods)


def _ffn_up_kernel(h_ref, wg_ref, wu_ref, o_ref, wg_s, wu_s):
    @pl.when(pl.program_id(1) == 0)
    def _():
        wg_s[...] = wg_ref[...].astype(BF16)
        wu_s[...] = wu_ref[...].astype(BF16)

    h = h_ref[...]
    g = jnp.dot(h, wg_s[...], preferred_element_type=F32)
    u = jnp.dot(h, wu_s[...], preferred_element_type=F32)
    o_ref[...] = (g * _sigmoid(g) * u).astype(o_ref.dtype)


def _ffn_up(h, wg, wu, layer):
    m, k = h.shape
    n = wg.shape[2]
    tm = ROW_TILE
    tn = 512
    w_spec = pl.BlockSpec((None, k, tn), lambda j, i: (layer, 0, j))
    return pl.pallas_call(
        _ffn_up_kernel,
        grid=(n // tn, m // tm),
        in_specs=[pl.BlockSpec((tm, k), lambda j, i: (i, 0)), w_spec, w_spec],
        out_specs=pl.BlockSpec((tm, tn), lambda j, i: (i, j)),
        out_shape=jax.ShapeDtypeStruct((m, n), BF16),
        scratch_shapes=[pltpu.VMEM((k, tn), BF16), pltpu.VMEM((k, tn), BF16)],
        compiler_params=_cparams("arbitrary", "arbitrary"),
        name="ffn_up",
    )(h, wg, wu)


def _ffn_down_kernel(a_ref, w_ref, x_ref, g_ref, gate_ref, o_ref, acc_ref):
    kk = pl.program_id(1)

    @pl.when(kk == 0)
    def _():
        acc_ref[...] = jnp.zeros_like(acc_ref)

    acc_ref[...] += jnp.dot(a_ref[...], w_ref[...], preferred_element_type=F32)

    @pl.when(kk == pl.num_programs(1) - 1)
    def _():
        o_ref[...] = x_ref[...] + gate_ref[...] * _rms(acc_ref[...], g_ref[...])


def _ffn_down(act, w, x, g, mods, k_gate, cond_row):
    m, d = x.shape
    kdim = act.shape[1]
    tm = ROW_TILE
    tk = kdim // 4
    return pl.pallas_call(
        _ffn_down_kernel,
        grid=(m // tm, kdim // tk),
        in_specs=[pl.BlockSpec((tm, tk), lambda i, k: (i, k)),
                  pl.BlockSpec((tk, d), lambda i, k: (k, 0)),
                  pl.BlockSpec((tm, d), lambda i, k: (i, 0)),
                  pl.BlockSpec((1, d), lambda i, k: (0, 0)),
                  _mod_spec(k_gate, cond_row)],
        out_specs=pl.BlockSpec((tm, d), lambda i, k: (i, 0)),
        out_shape=jax.ShapeDtypeStruct((m, d), F32),
        scratch_shapes=[pltpu.VMEM((tm, d), F32)],
        compiler_params=_cparams("arbitrary", "arbitrary"),
        name="ffn_down",
    )(act, w, x, g.reshape(1, d), mods)


def _rope_tables(n_tok):
    t = jnp.arange(n_tok, dtype=jnp.int32)
    row = (t // GRID_W).astype(F32)
    col = (t % GRID_W).astype(F32)
    n_freq = MLA_ROPE // 4
    inv = ROPE_BASE ** (-jnp.arange(n_freq, dtype=F32) / n_freq)
    ang = jnp.stack([row[:, None] * inv, col[:, None] * inv], axis=1)
    cos = jnp.broadcast_to(jnp.cos(ang)[:, :, None, :], (n_tok, 2, 2, n_freq))
    sin = jnp.sin(ang)
    sin = jnp.stack([-sin, sin], axis=2)
    return cos.reshape(n_tok, MLA_ROPE), sin.reshape(n_tok, MLA_ROPE)


def _half_swap_perm():
    n_freq = MLA_ROPE // 4
    return np.arange(MLA_ROPE).reshape(2, 2, n_freq)[:, ::-1, :].reshape(MLA_ROPE)


def _layer_weights(w_in, w_q_b, w_kv_b):
    perm = _half_swap_perm()
    o = np.cumsum((0, MLA_Q_RANK, MLA_KV_RANK, MLA_ROPE, RG_WIDTH, RG_WIDTH, NA_WIDTH, NA_WIDTH, NA_WIDTH,
                   D_MODEL, D_MODEL, D_MODEL))
    w_a = jnp.concatenate([w_in[:, :o[3]], w_in[:, o[2]:o[3]][:, perm]], axis=1).astype(BF16)
    w_rg = w_in[:, o[3]:o[5]].astype(BF16)
    w_na_qk = jnp.concatenate([w_in[:, o[5]:o[6]] * (NA_SCALE * LOG2E), w_in[:, o[6]:o[7]]], axis=1).astype(BF16)
    w_na_vt = w_in[:, o[7]:o[8]].T.astype(BF16)
    w_gt = w_in[:, o[8]:o[11]].astype(BF16)
    wq = w_q_b.reshape(MLA_Q_RANK, MLA_HEADS, MLA_QK)
    wq_rope = wq[:, :, MLA_NOPE:]
    wq = jnp.concatenate([wq[:, :, :MLA_NOPE].reshape(MLA_Q_RANK, -1),
                          wq_rope.reshape(MLA_Q_RANK, -1),
                          wq_rope[:, :, perm].reshape(MLA_Q_RANK, -1)], axis=1)
    wq = (wq * (MLA_SCALE * LOG2E)).astype(BF16)
    wkv = w_kv_b.reshape(MLA_KV_RANK, MLA_HEADS, MLA_NOPE + MLA_V)
    wk = wkv[:, :, :MLA_NOPE].reshape(MLA_KV_RANK, -1).astype(BF16)
    wvt = wkv[:, :, MLA_NOPE:].reshape(MLA_KV_RANK, -1).T.astype(BF16)
    return w_a, w_rg, w_na_qk, w_na_vt, w_gt, wq, wk, wvt


def kernel(x, c, ctx, c_ctx, w_ada, b_ada, g_mix_pre, g_mix_post, g_ffn_pre, g_ffn_post, w_in, g_q_a, w_q_b, g_kv_a, w_kv_b, w_mla_o, conv_w, conv_b, rg_wa, rg_ba, rg_wx, rg_bx, rg_lambda, w_rg_o, na_rpb, w_na_o, w_out, w_ffn_gate, w_ffn_up, w_ffn_down):
    batch, seq, d = x.shape
    ctx_len = ctx.shape[1]
    depth = w_ada.shape[0]
    n_rows = seq // GRID_W
    assert d == D_MODEL and batch + 1 <= COND_ROWS
    assert seq % ROW_TILE == 0 and (batch * ctx_len) % ROW_TILE == 0 and seq % KEY_CHUNK == 0
    assert seq % GRID_W == 0 and n_rows % NA_ROWS_PER_STEP == 0 and n_rows >= 2 * NA_KEY_ROWS
    lat_row = lambda i: i // (seq // ROW_TILE)
    ctx_row = lambda i: batch
    hq, hv = MLA_HEADS, MLA_HEADS * MLA_V
    nh, hd = NA_HEADS, NA_HEAD_DIM

    cond = jnp.zeros((COND_ROWS, d), F32).at[:batch].set(c).at[batch].set(c_ctx)
    mods = _ada(cond, w_ada, b_ada).reshape(depth, COND_ROWS, 6, 1, d)

    cos_l, sin_l = _rope_tables(seq)
    cos_c = jnp.ones((batch * ctx_len, MLA_ROPE), F32)
    sin_c = jnp.zeros((batch * ctx_len, MLA_ROPE), F32)

    xl = x.reshape(batch * seq, d)
    xc = ctx.reshape(batch * ctx_len, d)
    ctx_plan = [(0, 0, ctx_len)]

    for l in range(depth):
        last = l == depth - 1
        md = mods[l]
        w_a, w_rg, w_na_qk, w_na_vt, w_gt, wq, wk, wvt = _layer_weights(w_in[l], w_q_b[l], w_kv_b[l])

        hl = _modulate(xl, g_mix_pre[l], md, 0, 1, lat_row)
        hc = _modulate(xc, g_mix_pre[l], md, 0, 1, ctx_row)
        a_l = _matmul(hl, w_a, F32, w_a.shape[1], name="proj_mla")
        a_c = _matmul(hc, w_a, F32, w_a.shape[1], name="proj_mla")
        rg_l = _matmul(hl, w_rg, F32, 1024, name="proj_rg")
        rg_c = _matmul(hc, w_rg, F32, 1024, name="proj_rg")
        nqk_l = _matmul(hl, w_na_qk, BF16, 1024, name="proj_na_qk")
        nqk_c = _matmul(hc, w_na_qk, BF16, 1024, name="proj_na_qk")
        nvt_l = _matmul_nt(w_na_vt, hl, BF16, 1024, name="proj_na_vt")
        nvt_c = _matmul_nt(w_na_vt, hc, BF16, 1024, name="proj_na_vt")
        gt_l = _matmul(hl, w_gt, BF16, 1024, act="sigmoid", name="proj_gate")

        q_l, k_l, vt_l = _mla_qkv(a_l, g_q_a[l], g_kv_a[l], wq, wk, wvt, cos_l, sin_l)
        q_c, k_c, vt_c = _mla_qkv(a_c, g_q_a[l], g_kv_a[l], wq, wk, wvt, cos_c, sin_c)
        mla_l = _mla_attn(q_l, k_c, k_l, vt_c, vt_l, batch, seq, ctx_len)

        rgo_l, rgo_c = _rglru(rg_l, rg_c, conv_w[l], conv_b[l], rg_wa[l].astype(BF16), rg_ba[l],
                              rg_wx[l].astype(BF16), rg_bx[l], rg_lambda[l], batch, seq, ctx_len)

        nao_l = _natten(nqk_l, nvt_l, nqk_c, nvt_c, _natten_rpb_tiles(na_rpb[l]), batch, seq, ctx_len)

        wm, wr, wn, wo = (w_mla_o[l].astype(BF16), w_rg_o[l].astype(BF16), w_na_o[l].astype(BF16),
                          w_out[l].astype(BF16))
        wfd = w_ffn_down[l].astype(BF16)

        def tail(xs, mla_o, rg_o, na_o, gates, cond_row):
            y = _merge(mla_o, rg_o, na_o, gates, wm, wr, wn)
            xs = _proj_residual(y, wo, xs, g_mix_post[l], md, 2, cond_row)
            h2 = _modulate(xs, g_ffn_pre[l], md, 3, 4, cond_row)
            act = _ffn_up(h2, w_ffn_gate, w_ffn_up, l)
            return _ffn_down(act, wfd, xs, g_ffn_post[l], md, 5, cond_row)

        xl = tail(xl, mla_l, rgo_l, nao_l, gt_l, lat_row)

        if not last:
            gt_c = _matmul(hc, w_gt, BF16, 1024, act="sigmoid", name="proj_gate")
            mla_c = _attend(
                (batch, hq), ctx_plan, q_c,
                pl.BlockSpec((None, ctx_len, MLA_QK), lambda b, h: (h, b, 0)),
                [k_c], [pl.BlockSpec((None, ctx_len, MLA_QK), lambda b, h: (h, b, 0))],
                [vt_c], [pl.BlockSpec((MLA_V, ctx_len), lambda b, h: (h, b))],
                jax.ShapeDtypeStruct((batch * ctx_len, hv), BF16),
                pl.BlockSpec((ctx_len, MLA_V), lambda b, h: (b, h)),
                "mla_attn_ctx")
            nao_c = _attend(
                (batch, nh), ctx_plan, nqk_c,
                pl.BlockSpec((ctx_len, hd), lambda b, h: (b, h)),
                [nqk_c], [pl.BlockSpec((ctx_len, hd), lambda b, h: (b, nh + h))],
                [nvt_c], [pl.BlockSpec((hd, ctx_len), lambda b, h: (h, b))],
                jax.ShapeDtypeStruct((batch * ctx_len, NA_WIDTH), BF16),
                pl.BlockSpec((ctx_len, hd), lambda b, h: (b, h)),
                "na_attn_ctx")
            xc = tail(xc, mla_c, rgo_c, nao_c, gt_c, ctx_row)

    return xl.reshape(batch, seq, d)
```

```python
import functools
import math

import numpy as np
import jax
import jax.numpy as jnp
from jax import lax
from jax.experimental import pallas as pl
from jax.experimental.pallas import tpu as pltpu

F32 = jnp.float32
BF16 = jnp.bfloat16

D_MODEL = 2048
GRID_W = 64
MLA_HEADS = 8
MLA_Q_RANK = 512
MLA_KV_RANK = 256
MLA_NOPE = 128
MLA_ROPE = 64
MLA_V = 128
MLA_QK = MLA_NOPE + MLA_ROPE
MLA_SCALE = 1.0 / math.sqrt(MLA_NOPE + MLA_ROPE)
ROPE_BASE = 10000.0
RG_WIDTH = 1024
RG_BLOCKS = 8
RG_BS = RG_WIDTH // RG_BLOCKS
RG_CONV = 4
RG_C = 8.0
NA_HEADS = 8
NA_HEAD_DIM = 128
NA_WIDTH = NA_HEADS * NA_HEAD_DIM
NA_WIN_ROWS = 8
NA_WIN_COLS = 16
NA_SCALE = 1.0 / math.sqrt(NA_HEAD_DIM)
NORM_EPS = 1e-6
NEG_INF = -1e30
LOG2E = math.log2(math.e)

SUBLANES = 8
ROW_TILE = 512
KEY_CHUNK = 512
COND_ROWS = 8
NA_ROWS_PER_STEP = 8
NA_KEY_ROWS = 16
NA_KEY_BLOCK = 4
SEG_PAD = 4
VMEM_LIMIT = 48 * 1024 * 1024

NT_DIMS = (((1,), (1,)), ((), ()))


def _cparams(*sem):
    return pltpu.CompilerParams(dimension_semantics=sem, vmem_limit_bytes=VMEM_LIMIT)


def _rms(x, g):
    return x * lax.rsqrt(jnp.mean(x * x, axis=-1, keepdims=True) + NORM_EPS) * g


def _sigmoid(x):
    return 0.5 * jnp.tanh(0.5 * x) + 0.5


def _ada_kernel(c_ref, w_ref, b_ref, o_ref):
    c = c_ref[...]
    s = (c * jax.nn.sigmoid(c)).astype(BF16)
    o_ref[...] = jnp.dot(s, w_ref[...].astype(BF16), preferred_element_type=F32) + b_ref[...]


def _ada(cond, w_ada, b_ada):
    depth, d, n = w_ada.shape
    tn = 1024
    return pl.pallas_call(
        _ada_kernel,
        grid=(depth, n // tn),
        in_specs=[pl.BlockSpec((COND_ROWS, d), lambda l, j: (0, 0)),
                  pl.BlockSpec((None, d, tn), lambda l, j: (l, 0, j)),
                  pl.BlockSpec((None, 1, tn), lambda l, j: (l, 0, j))],
        out_specs=pl.BlockSpec((None, COND_ROWS, tn), lambda l, j: (l, 0, j)),
        out_shape=jax.ShapeDtypeStruct((depth, COND_ROWS, n), F32),
        compiler_params=_cparams("arbitrary", "arbitrary"),
        name="ada",
    )(cond, w_ada, b_ada.reshape(depth, 1, n))


def _mod_spec(chunk, cond_row):
    return pl.BlockSpec((None, None, 1, D_MODEL), lambda i, *_: (cond_row(i), chunk, 0, 0))


def _modulated(x, g, shift, scale):
    return (_rms(x, g) * (1.0 + scale) + shift).astype(BF16)


def _modulate_kernel(x_ref, g_ref, sh_ref, sc_ref, o_ref):
    o_ref[...] = _modulated(x_ref[...], g_ref[...], sh_ref[...], sc_ref[...])


def _modulate(x, g, mods, k_shift, k_scale, cond_row):
    m, d = x.shape
    tm = ROW_TILE
    return pl.pallas_call(
        _modulate_kernel,
        grid=(m // tm,),
        in_specs=[pl.BlockSpec((tm, d), lambda i: (i, 0)),
                  pl.BlockSpec((1, d), lambda i: (0, 0)),
                  _mod_spec(k_shift, cond_row),
                  _mod_spec(k_scale, cond_row)],
        out_specs=pl.BlockSpec((tm, d), lambda i: (i, 0)),
        out_shape=jax.ShapeDtypeStruct((m, d), BF16),
        compiler_params=_cparams("arbitrary"),
        name="modulate",
    )(x, g.reshape(1, d), mods, mods)


def _mm_kernel(a_ref, b_ref, o_ref, *, act):
    acc = jnp.dot(a_ref[...], b_ref[...], preferred_element_type=F32)
    if act == "sigmoid":
        acc = _sigmoid(acc)
    o_ref[...] = acc.astype(o_ref.dtype)


def _matmul(a, b, layer, out_dtype, tn, act=None, name="matmul"):
    m, k = a.shape
    n = b.shape[2]
    tm = ROW_TILE
    return pl.pallas_call(
        functools.partial(_mm_kernel, act=act),
        grid=(n // tn, m // tm),
        in_specs=[pl.BlockSpec((tm, k), lambda j, i: (i, 0)),
                  pl.BlockSpec((None, k, tn), lambda j, i: (layer, 0, j))],
        out_specs=pl.BlockSpec((tm, tn), lambda j, i: (i, j)),
        out_shape=jax.ShapeDtypeStruct((m, n), out_dtype),
        compiler_params=_cparams("arbitrary", "arbitrary"),
        name=name,
    )(a, b)


def _mm_nt_kernel(w_ref, x_ref, o_ref):
    o_ref[...] = lax.dot_general(w_ref[...], x_ref[...], NT_DIMS, preferred_element_type=F32).astype(o_ref.dtype)


def _matmul_nt(wt, layer, x, out_dtype, tn, name):
    _, n, k = wt.shape
    m = x.shape[0]
    tm = ROW_TILE
    return pl.pallas_call(
        _mm_nt_kernel,
        grid=(n // tn, m // tm),
        in_specs=[pl.BlockSpec((None, tn, k), lambda j, i: (layer, j, 0)),
                  pl.BlockSpec((tm, k), lambda j, i: (i, 0))],
        out_specs=pl.BlockSpec((tn, tm), lambda j, i: (j, i)),
        out_shape=jax.ShapeDtypeStruct((n, m), out_dtype),
        compiler_params=_cparams("arbitrary", "arbitrary"),
        name=name,
    )(wt, x)


def _mla_qkv_kernel(a_ref, gq_ref, gkv_ref, wq_ref, wk_ref, wvt_ref, c_ref, s_ref, q_ref, k_ref, vt_ref):
    a = a_ref[...]
    qa = _rms(a[:, :MLA_Q_RANK], gq_ref[...]).astype(BF16)
    kva = _rms(a[:, MLA_Q_RANK:MLA_Q_RANK + MLA_KV_RANK], gkv_ref[...]).astype(BF16)
    q = jnp.dot(qa, wq_ref[...], preferred_element_type=F32)
    kn = jnp.dot(kva, wk_ref[...], preferred_element_type=F32)
    vt_ref[...] = lax.dot_general(wvt_ref[...], kva, NT_DIMS, preferred_element_type=F32).astype(BF16)
    cos = c_ref[...]
    sin = s_ref[...]
    o_kr = MLA_Q_RANK + MLA_KV_RANK
    kr = (a[:, o_kr:o_kr + MLA_ROPE] * cos + a[:, o_kr + MLA_ROPE:o_kr + 2 * MLA_ROPE] * sin).astype(BF16)
    n_nope = MLA_HEADS * MLA_NOPE
    n_rope = MLA_HEADS * MLA_ROPE
    for h in range(MLA_HEADS):
        q_ref[h, :, 0:MLA_NOPE] = q[:, h * MLA_NOPE:(h + 1) * MLA_NOPE].astype(BF16)
        r0 = n_nope + h * MLA_ROPE
        qr = q[:, r0:r0 + MLA_ROPE] * cos + q[:, r0 + n_rope:r0 + n_rope + MLA_ROPE] * sin
        q_ref[h, :, MLA_NOPE:MLA_QK] = qr.astype(BF16)
        k_ref[h, :, 0:MLA_NOPE] = kn[:, h * MLA_NOPE:(h + 1) * MLA_NOPE].astype(BF16)
        k_ref[h, :, MLA_NOPE:MLA_QK] = kr


def _mla_qkv(a, gq, gkv, wq, wk, wvt, cos, sin):
    m, wa = a.shape
    tm = ROW_TILE
    nt = cos.shape[0] // tm
    hv = MLA_HEADS * MLA_V
    return pl.pallas_call(
        _mla_qkv_kernel,
        grid=(m // tm,),
        in_specs=[pl.BlockSpec((tm, wa), lambda i: (i, 0)),
                  pl.BlockSpec((1, MLA_Q_RANK), lambda i: (0, 0)),
                  pl.BlockSpec((1, MLA_KV_RANK), lambda i: (0, 0)),
                  pl.BlockSpec(wq.shape, lambda i: (0, 0)),
                  pl.BlockSpec(wk.shape, lambda i: (0, 0)),
                  pl.BlockSpec(wvt.shape, lambda i: (0, 0)),
                  pl.BlockSpec((tm, MLA_ROPE), lambda i: (i % nt, 0)),
                  pl.BlockSpec((tm, MLA_ROPE), lambda i: (i % nt, 0))],
        out_specs=[pl.BlockSpec((MLA_HEADS, tm, MLA_QK), lambda i: (0, i, 0)),
                   pl.BlockSpec((MLA_HEADS, tm, MLA_QK), lambda i: (0, i, 0)),
                   pl.BlockSpec((hv, tm), lambda i: (0, i))],
        out_shape=[jax.ShapeDtypeStruct((MLA_HEADS, m, MLA_QK), BF16),
                   jax.ShapeDtypeStruct((MLA_HEADS, m, MLA_QK), BF16),
                   jax.ShapeDtypeStruct((hv, m), BF16)],
        compiler_params=_cparams("arbitrary"),
        name="mla_qkv",
    )(a, gq.reshape(1, -1), gkv.reshape(1, -1), wq, wk, wvt, cos, sin)


def _flash_t(q, chunks):
    scores = []
    m = None
    for k, _, bias_t in chunks:
        s = lax.dot_general(k, q, NT_DIMS, preferred_element_type=F32)
        if bias_t is not None:
            s = s + bias_t
        scores.append(s)
        mc = s.max(axis=0, keepdims=True)
        m = mc if m is None else jnp.maximum(m, mc)
    den = acc = None
    for s, (_, vt, _) in zip(scores, chunks):
        p = jnp.exp2(s - m)
        ps = p.sum(axis=0, keepdims=True)
        pv = jnp.dot(vt, p.astype(BF16), preferred_element_type=F32)
        den = ps if den is None else den + ps
        acc = pv if acc is None else acc + pv
    return acc / den


def _attn_pipeline(n_tiles, load_q, chunks_of, store_out, s_bufs, m_bufs):
    def scores(q, chunk, s_buf, off, m):
        n, load_k, _, load_bias = chunk
        s = lax.dot_general(load_k(), q, NT_DIMS, preferred_element_type=F32)
        if load_bias is not None:
            s = s + load_bias()
        s_buf[off:off + n, :] = s
        mc = s.max(axis=0, keepdims=True)
        return mc if m is None else jnp.maximum(m, mc)

    def values(chunk, s_buf, off, m, den, acc):
        n, _, load_vt, _ = chunk
        p = jnp.exp2(s_buf[off:off + n, :] - m)
        ps = p.sum(axis=0, keepdims=True)
        pv = jnp.dot(load_vt(), p.astype(BF16), preferred_element_type=F32)
        return (ps if den is None else den + ps), (pv if acc is None else acc + pv)

    q0 = load_q(0)
    m0, off = None, 0
    for chunk in chunks_of(0):
        m0 = scores(q0, chunk, s_bufs[0], off, m0)
        off += chunk[0]
    m_bufs[0][...] = m0

    def step(t, cur, nxt):
        t_next = jnp.minimum(t + 1, n_tiles - 1)
        q_next = load_q(t_next)
        m = m_bufs[cur][...]
        m_next = den = acc = None
        off = 0
        for c_cur, c_next in zip(chunks_of(t), chunks_of(t_next)):
            m_next = scores(q_next, c_next, s_bufs[nxt], off, m_next)
            den, acc = values(c_cur, s_bufs[cur], off, m, den, acc)
            off += c_cur[0]
        m_bufs[nxt][...] = m_next
        store_out(t, acc / den)

    def two_steps(i, carry):
        step(2 * i, 0, 1)
        step(2 * i + 1, 1, 0)
        return carry

    assert n_tiles % 2 == 0
    lax.fori_loop(0, n_tiles // 2, two_steps, 0)


def _mla_attn_kernel(q_ref, kc_ref, kl_ref, vtc_ref, vtl_ref, o_ref, s_a, s_b, m_a, m_b, *, tq, chunk):
    seq = kl_ref.shape[0]
    chunks = [(kc_ref.shape[0], lambda: kc_ref[...], lambda: vtc_ref[...], None)]
    for c0 in range(0, seq, chunk):
        chunks.append((chunk, lambda c0=c0: kl_ref[c0:c0 + chunk, :], lambda c0=c0: vtl_ref[:, c0:c0 + chunk], None))

    def load_q(t):
        return q_ref[pl.ds(pl.multiple_of(t * tq, tq), tq), :]

    def store_out(t, out_t):
        o_ref[pl.ds(pl.multiple_of(t * tq, tq), tq), :] = out_t.T.astype(o_ref.dtype)

    _attn_pipeline(seq // tq, load_q, lambda t: chunks, store_out, (s_a, s_b), (m_a, m_b))


def _mla_attn(q_l, k_c, k_l, vt_c, vt_l, batch, seq, ctx_len):
    tq = ROW_TILE
    n_keys = seq + ctx_len
    return pl.pallas_call(
        functools.partial(_mla_attn_kernel, tq=tq, chunk=KEY_CHUNK),
        grid=(batch, MLA_HEADS),
        in_specs=[pl.BlockSpec((None, seq, MLA_QK), lambda b, h: (h, b, 0)),
                  pl.BlockSpec((None, ctx_len, MLA_QK), lambda b, h: (h, b, 0)),
                  pl.BlockSpec((None, seq, MLA_QK), lambda b, h: (h, b, 0)),
                  pl.BlockSpec((MLA_V, ctx_len), lambda b, h: (h, b)),
                  pl.BlockSpec((MLA_V, seq), lambda b, h: (h, b))],
        out_specs=pl.BlockSpec((seq, MLA_V), lambda b, h: (b, h)),
        out_shape=jax.ShapeDtypeStruct((batch * seq, MLA_HEADS * MLA_V), BF16),
        scratch_shapes=[pltpu.VMEM((n_keys, tq), F32), pltpu.VMEM((n_keys, tq), F32),
                        pltpu.VMEM((1, tq), F32), pltpu.VMEM((1, tq), F32)],
        compiler_params=_cparams("arbitrary", "arbitrary"),
        name="mla_attn",
    )(q_l, k_c, k_l, vt_c, vt_l)


def _attn_kernel(*refs, plan, nseg):
    q_ref = refs[0]
    k_refs = refs[1:1 + nseg]
    vt_refs = refs[1 + nseg:1 + 2 * nseg]
    o_ref = refs[1 + 2 * nseg]
    chunks = [(k_refs[s][c0:c0 + n, :], vt_refs[s][:, c0:c0 + n], None) for s, c0, n in plan]
    o_ref[...] = _flash_t(q_ref[...], chunks).T.astype(o_ref.dtype)


def _attend(grid, plan, q, q_spec, ks, k_specs, vts, vt_specs, out_shape, out_spec, name):
    nseg = len(ks)
    return pl.pallas_call(
        functools.partial(_attn_kernel, plan=tuple(plan), nseg=nseg),
        grid=grid,
        in_specs=[q_spec] + list(k_specs) + list(vt_specs),
        out_specs=out_spec,
        out_shape=out_shape,
        compiler_params=_cparams(*(["arbitrary"] * len(grid))),
        name=name,
    )(q, *ks, *vts)


def _natten_row_structure(n_rows):
    def block(r0):
        ws = np.clip(r0 - NA_WIN_ROWS // 2, 0, n_rows - NA_KEY_ROWS)
        r = r0 + np.arange(NA_ROWS_PER_STEP)
        start = np.clip(r - NA_WIN_ROWS // 2, 0, n_rows - NA_WIN_ROWS)
        key_row = ws + np.arange(NA_KEY_ROWS)
        kr = key_row[None, :] - start[:, None]
        valid = (kr >= 0) & (kr < NA_WIN_ROWS)
        dr = key_row[None, :] - r[:, None] + (NA_WIN_ROWS - 1)
        return valid, np.where(valid, dr, 0)

    blocks = [block(r0) for r0 in range(0, n_rows, NA_ROWS_PER_STEP)]
    for valid, dr in blocks[2:-1]:
        assert (valid == blocks[1][0]).all() and (dr == blocks[1][1]).all()
    return blocks[0], blocks[1], blocks[-1]


def _natten_kernel(q_ref, k_ref, vt_ref, kc_ref, vtc_ref, rpb_ref, o_ref, bias, s_a, s_b, m_a, m_b, *, structure):
    @pl.when(pl.program_id(1) == 0)
    def _():
        masked = jnp.full((GRID_W, GRID_W), NEG_INF, F32)
        for v, (valid, dr) in enumerate(structure):
            for i in range(NA_KEY_ROWS):
                tiles = [rpb_ref[int(dr[j, i])] if valid[j, i] else masked for j in range(NA_ROWS_PER_STEP)]
                bias[v, i * GRID_W:(i + 1) * GRID_W, :] = jnp.concatenate(tiles, axis=1)

    seq = k_ref.shape[0]
    n_rows = seq // GRID_W
    tq = NA_ROWS_PER_STEP * GRID_W
    n_tiles = seq // tq
    half = NA_KEY_ROWS * GRID_W // 2
    align = NA_KEY_BLOCK * GRID_W

    def chunks_of(t):
        variant = jnp.where(t == 0, 0, jnp.where(t == n_tiles - 1, 2, 1))
        first_row = jnp.clip(t * NA_ROWS_PER_STEP - NA_WIN_ROWS // 2, 0, n_rows - NA_KEY_ROWS)
        chunks = [(kc_ref.shape[0], lambda: kc_ref[...], lambda: vtc_ref[...], None)]
        for c in range(2):
            k0 = pl.multiple_of(first_row * GRID_W + c * half, align)
            chunks.append((half,
                           lambda k0=k0: k_ref[pl.ds(k0, half), :],
                           lambda k0=k0: vt_ref[:, pl.ds(k0, half)],
                           lambda c=c: bias[variant, c * half:(c + 1) * half, :]))
        return chunks

    def load_q(t):
        return q_ref[pl.ds(pl.multiple_of(t * tq, tq), tq), :]

    def store_out(t, out_t):
        o_ref[pl.ds(pl.multiple_of(t * tq, tq), tq), :] = out_t.T.astype(o_ref.dtype)

    _attn_pipeline(n_tiles, load_q, chunks_of, store_out, (s_a, s_b), (m_a, m_b))


def _natten(qk_l, vt_l, qk_c, vt_c, rpb_tiles, batch, seq, ctx_len):
    n_rows = seq // GRID_W
    tq = NA_ROWS_PER_STEP * GRID_W
    n_keys = NA_KEY_ROWS * GRID_W + ctx_len
    hd = NA_HEAD_DIM
    nh = NA_HEADS
    return pl.pallas_call(
        functools.partial(_natten_kernel, structure=_natten_row_structure(n_rows)),
        grid=(nh, batch),
        in_specs=[pl.BlockSpec((seq, hd), lambda h, b: (b, h)),
                  pl.BlockSpec((seq, hd), lambda h, b: (b, nh + h)),
                  pl.BlockSpec((hd, seq), lambda h, b: (h, b)),
                  pl.BlockSpec((ctx_len, hd), lambda h, b: (b, nh + h)),
                  pl.BlockSpec((hd, ctx_len), lambda h, b: (h, b)),
                  pl.BlockSpec((None,) + rpb_tiles.shape[1:], lambda h, b: (h, 0, 0, 0))],
        out_specs=pl.BlockSpec((seq, hd), lambda h, b: (b, h)),
        out_shape=jax.ShapeDtypeStruct((batch * seq, NA_WIDTH), BF16),
        scratch_shapes=[pltpu.VMEM((3, NA_KEY_ROWS * GRID_W, tq), F32),
                        pltpu.VMEM((n_keys, tq), F32), pltpu.VMEM((n_keys, tq), F32),
                        pltpu.VMEM((1, tq), F32), pltpu.VMEM((1, tq), F32)],
        compiler_params=_cparams("arbitrary", "arbitrary"),
        name="natten",
    )(qk_l, qk_l, vt_l, qk_c, vt_c, rpb_tiles)


def _natten_rpb_tiles(rpb):
    col = np.arange(GRID_W)
    c_start = np.clip(col - NA_WIN_COLS // 2, 0, GRID_W - NA_WIN_COLS)
    col_in = (col[None, :] >= c_start[:, None]) & (col[None, :] < c_start[:, None] + NA_WIN_COLS)
    dc = np.clip(col[None, :] - col[:, None] + (NA_WIN_COLS - 1), 0, 2 * NA_WIN_COLS - 2)
    onehot = (dc.T[None] == np.arange(2 * NA_WIN_COLS - 1)[:, None, None]).astype(np.float32)
    t = jnp.einsum("hdc,ckq->hdkq", rpb.astype(F32), onehot, precision=lax.Precision.HIGHEST)
    return jnp.where(col_in.T[None, None], t * LOG2E, NEG_INF)


def _softplus(x):
    return jnp.maximum(x, 0.0) + jnp.log1p(jnp.exp(-jnp.abs(x)))


def _rg_kernel(ul_ref, uc_ref, gl_ref, gc_ref, cw_ref, cb_ref, wa_ref, ba_ref, wx_ref, bx_ref, lam_ref,
               ol_ref, oc_ref, upad, ucv, a_s, b_s, h_s, *, t_lat, t_ctx):
    pad = SUBLANES
    cw = cw_ref[...]
    cb = cb_ref[...]
    zeros_pad = jnp.zeros((pad, RG_BS), F32)
    seg_c = t_ctx // SUBLANES + SEG_PAD
    seg_l = t_lat // SUBLANES + SEG_PAD
    off_l = SUBLANES * seg_c

    def conv_stream(u_ref, t_len, dst, chunk):
        upad[0:pad, :] = zeros_pad
        upad[pad:pad + t_len, :] = u_ref[...]
        upad[pad + t_len:2 * pad + t_len, :] = zeros_pad
        for c0 in range(0, t_len, chunk):
            y = cb
            for k in range(RG_CONV):
                y = y + cw[k:k + 1, :] * upad[pad + c0 + k - RG_CONV // 2:pad + c0 + k - RG_CONV // 2 + chunk, :]
            ucv[dst + c0:dst + c0 + chunk, :] = y

    conv_stream(uc_ref, t_ctx, 0, t_ctx)
    conv_stream(ul_ref, t_lat, t_ctx, 512)

    rate = (-RG_C * LOG2E) * _softplus(-lam_ref[...])
    coef_chunk = 256

    def coef_range(src0, dst0, n_chunks):
        def coef(c, carry):
            r0 = pl.multiple_of(src0 + c * coef_chunk, SUBLANES)
            w0 = pl.multiple_of(dst0 + c * coef_chunk, SUBLANES)
            u = ucv[pl.ds(r0, coef_chunk), :]
            ub = u.astype(BF16)
            for d in range(2):
                r = _sigmoid(jnp.dot(ub, wa_ref[d], preferred_element_type=F32) + ba_ref[d:d + 1, :])
                i = _sigmoid(jnp.dot(ub, wx_ref[d], preferred_element_type=F32) + bx_ref[d:d + 1, :])
                a = jnp.exp2(r * rate[d:d + 1, :])
                a_s[d, pl.ds(w0, coef_chunk), :] = a
                b_s[d, pl.ds(w0, coef_chunk), :] = jnp.sqrt((1.0 - a) * (1.0 + a)) * (i * u)
            return carry
        lax.fori_loop(0, n_chunks, coef, 0)

    coef_range(0, 0, t_ctx // coef_chunk)
    coef_range(t_ctx, off_l, t_lat // coef_chunk)
    for d in range(2):
        for lo, hi in ((t_ctx, off_l), (off_l + t_lat, off_l + SUBLANES * seg_l)):
            a_s[d, lo:hi, :] = jnp.ones((hi - lo, RG_BS), F32)
            b_s[d, lo:hi, :] = jnp.zeros((hi - lo, RG_BS), F32)

    sub = lax.broadcasted_iota(jnp.int32, (SUBLANES, RG_BS), 0)

    def scan_stream(off, seg, h0_f, h0_b):
        def ld(ref, d, j):
            return ref[d, pl.ds(off + j, SUBLANES, stride=seg), :]

        def sweep1(j, st):
            hf, pf, hb, pb = st
            jb = seg - 1 - j
            af = ld(a_s, 0, j)
            ab = ld(a_s, 1, jb)
            return (af * hf + ld(b_s, 0, j), af * pf, ab * hb + ld(b_s, 1, jb), ab * pb)

        z = jnp.zeros((SUBLANES, RG_BS), F32)
        o = jnp.ones((SUBLANES, RG_BS), F32)
        hf, pf, hb, pb = lax.fori_loop(0, seg, sweep1, (z, o, z, o), unroll=4)

        start_f = z
        c = h0_f
        for s in range(SUBLANES):
            start_f = jnp.where(sub == s, c, start_f)
            c = pf[s:s + 1, :] * c + hf[s:s + 1, :]
        end_f = c
        start_b = z
        c = h0_b
        for s in range(SUBLANES - 1, -1, -1):
            start_b = jnp.where(sub == s, c, start_b)
            c = pb[s:s + 1, :] * c + hb[s:s + 1, :]
        end_b = c

        def sweep2(j, st):
            hf, hb = st
            jb = seg - 1 - j
            hf = ld(a_s, 0, j) * hf + ld(b_s, 0, j)
            hb = ld(a_s, 1, jb) * hb + ld(b_s, 1, jb)
            h_s[0, pl.ds(off + j, SUBLANES, stride=seg), :] = hf
            h_s[1, pl.ds(off + jb, SUBLANES, stride=seg), :] = hb
            return (hf, hb)

        lax.fori_loop(0, seg, sweep2, (start_f, start_b), unroll=4)
        return end_f, end_b

    zero_row = jnp.zeros((1, RG_BS), F32)
    end_f, end_b = scan_stream(0, seg_c, zero_row, zero_row)
    scan_stream(off_l, seg_l, end_f, end_b)

    oc_ref[...] = (jax.nn.gelu(gc_ref[...]) * (h_s[0, 0:t_ctx, :] + h_s[1, 0:t_ctx, :])).astype(oc_ref.dtype)
    out_chunk = 512

    def emit(c, carry):
        r0 = pl.multiple_of(c * out_chunk, out_chunk)
        rec = h_s[0, pl.ds(off_l + r0, out_chunk), :] + h_s[1, pl.ds(off_l + r0, out_chunk), :]
        ol_ref[pl.ds(r0, out_chunk), :] = (jax.nn.gelu(gl_ref[pl.ds(r0, out_chunk), :]) * rec).astype(ol_ref.dtype)
        return carry

    lax.fori_loop(0, t_lat // out_chunk, emit, 0)


def _rglru(rg_l, rg_c, conv_w, conv_b, wa, ba, wx, bx, lam, batch, seq, ctx_len):
    nb = RG_BLOCKS
    bs = RG_BS
    t_all = seq + ctx_len
    t_pad = t_all + 2 * SUBLANES * SEG_PAD
    blk = lambda b, j: (0, j)
    return pl.pallas_call(
        functools.partial(_rg_kernel, t_lat=seq, t_ctx=ctx_len),
        grid=(batch, nb),
        in_specs=[pl.BlockSpec((seq, bs), lambda b, j: (b, j)),
                  pl.BlockSpec((ctx_len, bs), lambda b, j: (b, j)),
                  pl.BlockSpec((seq, bs), lambda b, j: (b, nb + j)),
                  pl.BlockSpec((ctx_len, bs), lambda b, j: (b, nb + j)),
                  pl.BlockSpec((RG_CONV, bs), blk),
                  pl.BlockSpec((1, bs), blk),
                  pl.BlockSpec((2, None, bs, bs), lambda b, j: (0, j, 0, 0)),
                  pl.BlockSpec((2, bs), blk),
                  pl.BlockSpec((2, None, bs, bs), lambda b, j: (0, j, 0, 0)),
                  pl.BlockSpec((2, bs), blk),
                  pl.BlockSpec((2, bs), blk)],
        out_specs=[pl.BlockSpec((seq, bs), lambda b, j: (b, j)),
                   pl.BlockSpec((ctx_len, bs), lambda b, j: (b, j))],
        out_shape=[jax.ShapeDtypeStruct((batch * seq, RG_WIDTH), BF16),
                   jax.ShapeDtypeStruct((batch * ctx_len, RG_WIDTH), BF16)],
        scratch_shapes=[pltpu.VMEM((seq + 2 * SUBLANES, bs), F32),
                        pltpu.VMEM((t_all, bs), F32),
                        pltpu.VMEM((2, t_pad, bs), F32),
                        pltpu.VMEM((2, t_pad, bs), F32),
                        pltpu.VMEM((2, t_pad, bs), F32)],
        compiler_params=_cparams("arbitrary", "arbitrary"),
        name="rglru",
    )(rg_l, rg_c, rg_l, rg_c, conv_w, conv_b.reshape(1, -1), wa, ba, wx, bx, lam)


def _merge_kernel(m_ref, r_ref, n_ref, gm_ref, gr_ref, gn_ref, wm_ref, wr_ref, wn_ref, o_ref):
    y = gm_ref[...].astype(F32) * jnp.dot(m_ref[...], wm_ref[...], preferred_element_type=F32)
    y = y + gr_ref[...].astype(F32) * jnp.dot(r_ref[...], wr_ref[...], preferred_element_type=F32)
    y = y + gn_ref[...].astype(F32) * jnp.dot(n_ref[...], wn_ref[...], preferred_element_type=F32)
    o_ref[...] = y.astype(o_ref.dtype)


def _merge(mla_o, rg_o, na_o, gates, wm, wr, wn, layer):
    m, k = mla_o.shape
    n = wm.shape[2]
    tm = ROW_TILE
    tn = 1024
    nj = n // tn
    a_spec = pl.BlockSpec((tm, k), lambda j, i: (i, 0))
    w_spec = pl.BlockSpec((None, k, tn), lambda j, i: (layer, 0, j))
    return pl.pallas_call(
        _merge_kernel,
        grid=(nj, m // tm),
        in_specs=[a_spec, a_spec, a_spec,
                  pl.BlockSpec((tm, tn), lambda j, i: (i, j)),
                  pl.BlockSpec((tm, tn), lambda j, i: (i, nj + j)),
                  pl.BlockSpec((tm, tn), lambda j, i: (i, 2 * nj + j)),
                  w_spec, w_spec, w_spec],
        out_specs=pl.BlockSpec((tm, tn), lambda j, i: (i, j)),
        out_shape=jax.ShapeDtypeStruct((m, n), BF16),
        compiler_params=_cparams("arbitrary", "arbitrary"),
        name="merge",
    )(mla_o, rg_o, na_o, gates, gates, gates, wm, wr, wn)


def _proj_res_kernel(y_ref, w_ref, x_ref, g_ref, gate_ref, gn_ref, sh_ref, sc_ref, o_ref, h_ref):
    z = jnp.dot(y_ref[...], w_ref[...], preferred_element_type=F32)
    x_new = x_ref[...] + gate_ref[...] * _rms(z, g_ref[...])
    o_ref[...] = x_new
    h_ref[...] = _modulated(x_new, gn_ref[...], sh_ref[...], sc_ref[...])


def _proj_residual(y, w, layer, x, g, g_next, mods, k_gate, k_shift, k_scale, cond_row):
    m, d = x.shape
    tm = ROW_TILE
    row_spec = pl.BlockSpec((tm, d), lambda i: (i, 0))
    vec_spec = pl.BlockSpec((1, d), lambda i: (0, 0))
    return pl.pallas_call(
        _proj_res_kernel,
        grid=(m // tm,),
        in_specs=[pl.BlockSpec((tm, y.shape[1]), lambda i: (i, 0)),
                  pl.BlockSpec((None,) + w.shape[1:], lambda i: (layer, 0, 0)),
                  row_spec, vec_spec, _mod_spec(k_gate, cond_row),
                  vec_spec, _mod_spec(k_shift, cond_row), _mod_spec(k_scale, cond_row)],
        out_specs=[row_spec, row_spec],
        out_shape=[jax.ShapeDtypeStruct((m, d), F32), jax.ShapeDtypeStruct((m, d), BF16)],
        compiler_params=_cparams("arbitrary"),
        name="out_proj",
    )(y, w, x, g.reshape(1, d), mods, g_next.reshape(1, d), mods, mods)


def _ffn_up_kernel(h_ref, wg_ref, wu_ref, o_ref, wg_s, wu_s):
    @pl.when(pl.program_id(1) == 0)
    def _():
        wg_s[...] = wg_ref[...].astype(BF16)
        wu_s[...] = wu_ref[...].astype(BF16)

    h = h_ref[...]
    g = jnp.dot(h, wg_s[...], preferred_element_type=F32)
    u = jnp.dot(h, wu_s[...], preferred_element_type=F32)
    o_ref[...] = (g * _sigmoid(g) * u).astype(o_ref.dtype)


def _ffn_up(h, wg, wu, layer):
    m, k = h.shape
    n = wg.shape[2]
    tm = ROW_TILE
    tn = 512
    w_spec = pl.BlockSpec((None, k, tn), lambda j, i: (layer, 0, j))
    return pl.pallas_call(
        _ffn_up_kernel,
        grid=(n // tn, m // tm),
        in_specs=[pl.BlockSpec((tm, k), lambda j, i: (i, 0)), w_spec, w_spec],
        out_specs=pl.BlockSpec((tm, tn), lambda j, i: (i, j)),
        out_shape=jax.ShapeDtypeStruct((m, n), BF16),
        scratch_shapes=[pltpu.VMEM((k, tn), BF16), pltpu.VMEM((k, tn), BF16)],
        compiler_params=_cparams("arbitrary", "arbitrary"),
        name="ffn_up",
    )(h, wg, wu)


def _ffn_down_kernel(*refs, with_next):
    a_ref, w_ref, x_ref, g_ref, gate_ref = refs[:5]
    o_ref, acc_ref = (refs[8], refs[10]) if with_next else (refs[5], refs[6])
    kk = pl.program_id(1)

    @pl.when(kk == 0)
    def _():
        acc_ref[...] = jnp.zeros_like(acc_ref)

    acc_ref[...] += jnp.dot(a_ref[...], w_ref[...], preferred_element_type=F32)

    @pl.when(kk == pl.num_programs(1) - 1)
    def _():
        x_new = x_ref[...] + gate_ref[...] * _rms(acc_ref[...], g_ref[...])
        o_ref[...] = x_new
        if with_next:
            gn_ref, sh_ref, sc_ref, h_ref = refs[5], refs[6], refs[7], refs[9]
            h_ref[...] = _modulated(x_new, gn_ref[...], sh_ref[...], sc_ref[...])


def _ffn_down(act, w, layer, x, g, mods, k_gate, cond_row, next_layer=None):
    m, d = x.shape
    kdim = act.shape[1]
    tm = ROW_TILE
    tk = kdim // 4
    row_spec = pl.BlockSpec((tm, d), lambda i, k: (i, 0))
    vec_spec = pl.BlockSpec((1, d), lambda i, k: (0, 0))
    in_specs = [pl.BlockSpec((tm, tk), lambda i, k: (i, k)),
                pl.BlockSpec((None, tk, d), lambda i, k: (layer, k, 0)),
                row_spec, vec_spec, _mod_spec(k_gate, cond_row)]
    args = [act, w, x, g.reshape(1, d), mods]
    out_specs = [row_spec]
    out_shape = [jax.ShapeDtypeStruct((m, d), F32)]
    if next_layer is not None:
        g_next, mods_next = next_layer
        in_specs += [vec_spec, _mod_spec(0, cond_row), _mod_spec(1, cond_row)]
        args += [g_next.reshape(1, d), mods_next, mods_next]
        out_specs.append(row_spec)
        out_shape.append(jax.ShapeDtypeStruct((m, d), BF16))
    outs = pl.pallas_call(
        functools.partial(_ffn_down_kernel, with_next=next_layer is not None),
        grid=(m // tm, kdim // tk),
        in_specs=in_specs,
        out_specs=out_specs,
        out_shape=out_shape,
        scratch_shapes=[pltpu.VMEM((tm, d), F32)],
        compiler_params=_cparams("arbitrary", "arbitrary"),
        name="ffn_down",
    )(*args)
    return (outs[0], outs[1]) if next_layer is not None else (outs[0], None)


def _rope_tables(n_tok):
    t = jnp.arange(n_tok, dtype=jnp.int32)
    row = (t // GRID_W).astype(F32)
    col = (t % GRID_W).astype(F32)
    n_freq = MLA_ROPE // 4
    inv = ROPE_BASE ** (-jnp.arange(n_freq, dtype=F32) / n_freq)
    ang = jnp.stack([row[:, None] * inv, col[:, None] * inv], axis=1)
    cos = jnp.broadcast_to(jnp.cos(ang)[:, :, None, :], (n_tok, 2, 2, n_freq))
    sin = jnp.sin(ang)
    sin = jnp.stack([-sin, sin], axis=2)
    return cos.reshape(n_tok, MLA_ROPE), sin.reshape(n_tok, MLA_ROPE)


def _half_swap_perm():
    n_freq = MLA_ROPE // 4
    return np.arange(MLA_ROPE).reshape(2, 2, n_freq)[:, ::-1, :].reshape(MLA_ROPE)


IN_OFFSETS = np.cumsum((0, MLA_Q_RANK, MLA_KV_RANK, MLA_ROPE, RG_WIDTH, RG_WIDTH, NA_WIDTH, NA_WIDTH, NA_WIDTH,
                        D_MODEL, D_MODEL, D_MODEL))


def _w_in_prep_kernel(w_ref, krs_ref, wa_ref, wrg_ref, wqk_ref, wvt_ref, wgt_ref):
    o = IN_OFFSETS
    wa_ref[:, 0:o[3]] = w_ref[:, 0:o[3]].astype(BF16)
    wa_ref[:, o[3]:o[3] + MLA_ROPE] = krs_ref[...].astype(BF16)
    wrg_ref[...] = w_ref[:, o[3]:o[5]].astype(BF16)
    wqk_ref[:, 0:NA_WIDTH] = (w_ref[:, o[5]:o[6]] * (NA_SCALE * LOG2E)).astype(BF16)
    wqk_ref[:, NA_WIDTH:2 * NA_WIDTH] = w_ref[:, o[6]:o[7]].astype(BF16)
    wvt_ref[...] = w_ref[:, o[7]:o[8]].T.astype(BF16)
    wgt_ref[...] = w_ref[:, o[8]:o[11]].astype(BF16)


def _w_in_prep(w_in):
    depth, d, n_in = w_in.shape
    o = IN_OFFSETS
    tr = 128
    krs = w_in[:, :, o[2]:o[3]][:, :, _half_swap_perm()]
    widths = (o[3] + MLA_ROPE, 2 * RG_WIDTH, 2 * NA_WIDTH, 3 * D_MODEL)
    row_spec = lambda n: pl.BlockSpec((None, tr, n), lambda l, i: (l, i, 0))
    wa, wrg, wqk, wvt, wgt = pl.pallas_call(
        _w_in_prep_kernel,
        grid=(depth, d // tr),
        in_specs=[row_spec(n_in), row_spec(MLA_ROPE)],
        out_specs=[row_spec(widths[0]), row_spec(widths[1]), row_spec(widths[2]),
                   pl.BlockSpec((None, NA_WIDTH, tr), lambda l, i: (l, 0, i)),
                   row_spec(widths[3])],
        out_shape=[jax.ShapeDtypeStruct((depth, d, widths[0]), BF16),
                   jax.ShapeDtypeStruct((depth, d, widths[1]), BF16),
                   jax.ShapeDtypeStruct((depth, d, widths[2]), BF16),
                   jax.ShapeDtypeStruct((depth, NA_WIDTH, d), BF16),
                   jax.ShapeDtypeStruct((depth, d, widths[3]), BF16)],
        compiler_params=_cparams("arbitrary", "arbitrary"),
        name="w_in_prep",
    )(w_in, krs)
    return wa, wrg, wqk, wvt, wgt


def _cast_kernel(w_ref, o_ref):
    o_ref[...] = w_ref[...].astype(o_ref.dtype)


def _cast_bf16(w):
    depth, k, n = w.shape
    tr = 256
    spec = pl.BlockSpec((None, tr, n), lambda l, i: (l, i, 0))
    return pl.pallas_call(
        _cast_kernel,
        grid=(depth, k // tr),
        in_specs=[spec],
        out_specs=spec,
        out_shape=jax.ShapeDtypeStruct(w.shape, BF16),
        compiler_params=_cparams("arbitrary", "arbitrary"),
        name="cast_bf16",
    )(w)


def _mla_weights(w_q_b, w_kv_b):
    perm = _half_swap_perm()
    wq = w_q_b.reshape(MLA_Q_RANK, MLA_HEADS, MLA_QK)
    wq_rope = wq[:, :, MLA_NOPE:]
    wq = jnp.concatenate([wq[:, :, :MLA_NOPE].reshape(MLA_Q_RANK, -1),
                          wq_rope.reshape(MLA_Q_RANK, -1),
                          wq_rope[:, :, perm].reshape(MLA_Q_RANK, -1)], axis=1)
    wq = (wq * (MLA_SCALE * LOG2E)).astype(BF16)
    wkv = w_kv_b.reshape(MLA_KV_RANK, MLA_HEADS, MLA_NOPE + MLA_V)
    wk = wkv[:, :, :MLA_NOPE].reshape(MLA_KV_RANK, -1).astype(BF16)
    wvt = wkv[:, :, MLA_NOPE:].reshape(MLA_KV_RANK, -1).T.astype(BF16)
    return wq, wk, wvt


def kernel(x, c, ctx, c_ctx, w_ada, b_ada, g_mix_pre, g_mix_post, g_ffn_pre, g_ffn_post, w_in, g_q_a, w_q_b, g_kv_a, w_kv_b, w_mla_o, conv_w, conv_b, rg_wa, rg_ba, rg_wx, rg_bx, rg_lambda, w_rg_o, na_rpb, w_na_o, w_out, w_ffn_gate, w_ffn_up, w_ffn_down):
    batch, seq, d = x.shape
    ctx_len = ctx.shape[1]
    depth = w_ada.shape[0]
    n_rows = seq // GRID_W
    assert d == D_MODEL and batch + 1 <= COND_ROWS
    assert seq % ROW_TILE == 0 and (batch * ctx_len) % ROW_TILE == 0 and seq % KEY_CHUNK == 0
    assert seq % GRID_W == 0 and n_rows % NA_ROWS_PER_STEP == 0 and n_rows >= 2 * NA_KEY_ROWS
    lat_row = lambda i: i // (seq // ROW_TILE)
    ctx_row = lambda i: batch
    hq, hv = MLA_HEADS, MLA_HEADS * MLA_V
    nh, hd = NA_HEADS, NA_HEAD_DIM

    cond = jnp.zeros((COND_ROWS, d), F32).at[:batch].set(c).at[batch].set(c_ctx)
    mods = _ada(cond, w_ada, b_ada).reshape(depth, COND_ROWS, 6, 1, d)

    cos_l, sin_l = _rope_tables(seq)
    cos_c = jnp.ones((batch * ctx_len, MLA_ROPE), F32)
    sin_c = jnp.zeros((batch * ctx_len, MLA_ROPE), F32)

    w_a, w_rg, w_na_qk, w_na_vt, w_gt = _w_in_prep(w_in)
    wm, wr, wn, wo, wfd = (_cast_bf16(w) for w in (w_mla_o, w_rg_o, w_na_o, w_out, w_ffn_down))

    xl = x.reshape(batch * seq, d)
    xc = ctx.reshape(batch * ctx_len, d)
    ctx_plan = [(0, 0, ctx_len)]
    hl = _modulate(xl, g_mix_pre[0], mods[0], 0, 1, lat_row)
    hc = _modulate(xc, g_mix_pre[0], mods[0], 0, 1, ctx_row)

    for l in range(depth):
        last = l == depth - 1
        md = mods[l]
        next_layer = None if last else (g_mix_pre[l + 1], mods[l + 1])
        wq, wk, wvt = _mla_weights(w_q_b[l], w_kv_b[l])

        a_l = _matmul(hl, w_a, l, F32, w_a.shape[2], name="proj_mla")
        a_c = _matmul(hc, w_a, l, F32, w_a.shape[2], name="proj_mla")
        rg_l = _matmul(hl, w_rg, l, F32, 1024, name="proj_rg")
        rg_c = _matmul(hc, w_rg, l, F32, 1024, name="proj_rg")
        nqk_l = _matmul(hl, w_na_qk, l, BF16, 1024, name="proj_na_qk")
        nqk_c = _matmul(hc, w_na_qk, l, BF16, 1024, name="proj_na_qk")
        nvt_l = _matmul_nt(w_na_vt, l, hl, BF16, 1024, name="proj_na_vt")
        nvt_c = _matmul_nt(w_na_vt, l, hc, BF16, 1024, name="proj_na_vt")
        gt_l = _matmul(hl, w_gt, l, BF16, 1024, act="sigmoid", name="proj_gate")

        q_l, k_l, vt_l = _mla_qkv(a_l, g_q_a[l], g_kv_a[l], wq, wk, wvt, cos_l, sin_l)
        q_c, k_c, vt_c = _mla_qkv(a_c, g_q_a[l], g_kv_a[l], wq, wk, wvt, cos_c, sin_c)
        mla_l = _mla_attn(q_l, k_c, k_l, vt_c, vt_l, batch, seq, ctx_len)

        rgo_l, rgo_c = _rglru(rg_l, rg_c, conv_w[l], conv_b[l], rg_wa[l].astype(BF16), rg_ba[l],
                              rg_wx[l].astype(BF16), rg_bx[l], rg_lambda[l], batch, seq, ctx_len)

        nao_l = _natten(nqk_l, nvt_l, nqk_c, nvt_c, _natten_rpb_tiles(na_rpb[l]), batch, seq, ctx_len)

        def tail(xs, mla_o, rg_o, na_o, gates, cond_row):
            y = _merge(mla_o, rg_o, na_o, gates, wm, wr, wn, l)
            xs, h2 = _proj_residual(y, wo, l, xs, g_mix_post[l], g_ffn_pre[l], md, 2, 3, 4, cond_row)
            act = _ffn_up(h2, w_ffn_gate, w_ffn_up, l)
            return _ffn_down(act, wfd, l, xs, g_ffn_post[l], md, 5, cond_row, next_layer)

        xl, hl = tail(xl, mla_l, rgo_l, nao_l, gt_l, lat_row)

        if not last:
            gt_c = _matmul(hc, w_gt, l, BF16, 1024, act="sigmoid", name="proj_gate")
            mla_c = _attend(
                (batch, hq), ctx_plan, q_c,
                pl.BlockSpec((None, ctx_len, MLA_QK), lambda b, h: (h, b, 0)),
                [k_c], [pl.BlockSpec((None, ctx_len, MLA_QK), lambda b, h: (h, b, 0))],
                [vt_c], [pl.BlockSpec((MLA_V, ctx_len), lambda b, h: (h, b))],
                jax.ShapeDtypeStruct((batch * ctx_len, hv), BF16),
                pl.BlockSpec((ctx_len, MLA_V), lambda b, h: (b, h)),
                "mla_attn_ctx")
            nao_c = _attend(
                (batch, nh), ctx_plan, nqk_c,
                pl.BlockSpec((ctx_len, hd), lambda b, h: (b, h)),
                [nqk_c], [pl.BlockSpec((ctx_len, hd), lambda b, h: (b, nh + h))],
                [nvt_c], [pl.BlockSpec((hd, ctx_len), lambda b, h: (h, b))],
                jax.ShapeDtypeStruct((batch * ctx_len, NA_WIDTH), BF16),
                pl.BlockSpec((ctx_len, hd), lambda b, h: (b, h)),
                "na_attn_ctx")
            xc, hc = tail(xc, mla_c, rgo_c, nao_c, gt_c, ctx_row)

    return xl.reshape(batch, seq, d)
```

```python
import functools
import math

import numpy as np
import jax
import jax.numpy as jnp
from jax import lax
from jax.experimental import pallas as pl
from jax.experimental.pallas import tpu as pltpu

F32 = jnp.float32
BF16 = jnp.bfloat16

D_MODEL = 2048
GRID_W = 64
MLA_HEADS = 8
MLA_Q_RANK = 512
MLA_KV_RANK = 256
MLA_NOPE = 128
MLA_ROPE = 64
MLA_V = 128
MLA_QK = MLA_NOPE + MLA_ROPE
MLA_SCALE = 1.0 / math.sqrt(MLA_NOPE + MLA_ROPE)
ROPE_BASE = 10000.0
RG_WIDTH = 1024
RG_BLOCKS = 8
RG_BS = RG_WIDTH // RG_BLOCKS
RG_CONV = 4
RG_C = 8.0
NA_HEADS = 8
NA_HEAD_DIM = 128
NA_WIDTH = NA_HEADS * NA_HEAD_DIM
NA_WIN_ROWS = 8
NA_WIN_COLS = 16
NA_SCALE = 1.0 / math.sqrt(NA_HEAD_DIM)
NORM_EPS = 1e-6
NEG_INF = -1e30
LOG2E = math.log2(math.e)

SUBLANES = 8
ROW_TILE = 512
KEY_CHUNK = 512
COND_ROWS = 8
NA_ROWS_PER_STEP = 8
NA_KEY_ROWS = 16
NA_KEY_BLOCK = 4
SEG_PAD = 4
VMEM_LIMIT = 48 * 1024 * 1024

NT_DIMS = (((1,), (1,)), ((), ()))


def _cparams(*sem):
    return pltpu.CompilerParams(dimension_semantics=sem, vmem_limit_bytes=VMEM_LIMIT)


def _rms(x, g):
    return x * lax.rsqrt(jnp.mean(x * x, axis=-1, keepdims=True) + NORM_EPS) * g


def _sigmoid(x):
    return 0.5 * jnp.tanh(0.5 * x) + 0.5


def _ada_kernel(c_ref, w_ref, b_ref, o_ref):
    c = c_ref[...]
    s = (c * jax.nn.sigmoid(c)).astype(BF16)
    o_ref[...] = jnp.dot(s, w_ref[...].astype(BF16), preferred_element_type=F32) + b_ref[...]


def _ada(cond, w_ada, b_ada):
    depth, d, n = w_ada.shape
    tn = 1024
    return pl.pallas_call(
        _ada_kernel,
        grid=(depth, n // tn),
        in_specs=[pl.BlockSpec((COND_ROWS, d), lambda l, j: (0, 0)),
                  pl.BlockSpec((None, d, tn), lambda l, j: (l, 0, j)),
                  pl.BlockSpec((None, 1, tn), lambda l, j: (l, 0, j))],
        out_specs=pl.BlockSpec((None, COND_ROWS, tn), lambda l, j: (l, 0, j)),
        out_shape=jax.ShapeDtypeStruct((depth, COND_ROWS, n), F32),
        compiler_params=_cparams("arbitrary", "arbitrary"),
        name="ada",
    )(cond, w_ada, b_ada.reshape(depth, 1, n))


def _mod_spec(chunk, cond_row):
    return pl.BlockSpec((None, None, 1, D_MODEL), lambda i, *_: (cond_row(i), chunk, 0, 0))


def _modulated(x, g, shift, scale):
    return (_rms(x, g) * (1.0 + scale) + shift).astype(BF16)


def _modulate_kernel(x_ref, g_ref, sh_ref, sc_ref, o_ref):
    o_ref[...] = _modulated(x_ref[...], g_ref[...], sh_ref[...], sc_ref[...])


def _modulate(x, g, mods, k_shift, k_scale, cond_row):
    m, d = x.shape
    tm = ROW_TILE
    return pl.pallas_call(
        _modulate_kernel,
        grid=(m // tm,),
        in_specs=[pl.BlockSpec((tm, d), lambda i: (i, 0)),
                  pl.BlockSpec((1, d), lambda i: (0, 0)),
                  _mod_spec(k_shift, cond_row),
                  _mod_spec(k_scale, cond_row)],
        out_specs=pl.BlockSpec((tm, d), lambda i: (i, 0)),
        out_shape=jax.ShapeDtypeStruct((m, d), BF16),
        compiler_params=_cparams("arbitrary"),
        name="modulate",
    )(x, g.reshape(1, d), mods, mods)


def _mm_kernel(a_ref, b_ref, o_ref, *, act):
    acc = jnp.dot(a_ref[...], b_ref[...], preferred_element_type=F32)
    if act == "sigmoid":
        acc = _sigmoid(acc)
    o_ref[...] = acc.astype(o_ref.dtype)


def _matmul(a, b, layer, out_dtype, tn, act=None, name="matmul"):
    m, k = a.shape
    n = b.shape[2]
    tm = ROW_TILE
    return pl.pallas_call(
        functools.partial(_mm_kernel, act=act),
        grid=(n // tn, m // tm),
        in_specs=[pl.BlockSpec((tm, k), lambda j, i: (i, 0)),
                  pl.BlockSpec((None, k, tn), lambda j, i: (layer, 0, j))],
        out_specs=pl.BlockSpec((tm, tn), lambda j, i: (i, j)),
        out_shape=jax.ShapeDtypeStruct((m, n), out_dtype),
        compiler_params=_cparams("arbitrary", "arbitrary"),
        name=name,
    )(a, b)


def _mm_nt_kernel(w_ref, x_ref, o_ref):
    o_ref[...] = lax.dot_general(w_ref[...], x_ref[...], NT_DIMS, preferred_element_type=F32).astype(o_ref.dtype)


def _matmul_nt(wt, layer, x, out_dtype, tn, name):
    _, n, k = wt.shape
    m = x.shape[0]
    tm = ROW_TILE
    return pl.pallas_call(
        _mm_nt_kernel,
        grid=(n // tn, m // tm),
        in_specs=[pl.BlockSpec((None, tn, k), lambda j, i: (layer, j, 0)),
                  pl.BlockSpec((tm, k), lambda j, i: (i, 0))],
        out_specs=pl.BlockSpec((tn, tm), lambda j, i: (j, i)),
        out_shape=jax.ShapeDtypeStruct((n, m), out_dtype),
        compiler_params=_cparams("arbitrary", "arbitrary"),
        name=name,
    )(wt, x)


def _mla_qkv_kernel(a_ref, gq_ref, gkv_ref, wqt_ref, wk_ref, wvt_ref, c_ref, s_ref, ct_ref, st_ref,
                    qt_ref, k_ref, vt_ref):
    a = a_ref[...]
    qa = _rms(a[:, :MLA_Q_RANK], gq_ref[...]).astype(BF16)
    kva = _rms(a[:, MLA_Q_RANK:MLA_Q_RANK + MLA_KV_RANK], gkv_ref[...]).astype(BF16)
    qt = lax.dot_general(wqt_ref[...], qa, NT_DIMS, preferred_element_type=F32)
    kn = jnp.dot(kva, wk_ref[...], preferred_element_type=F32)
    vt_ref[...] = lax.dot_general(wvt_ref[...], kva, NT_DIMS, preferred_element_type=F32).astype(BF16)
    o_kr = MLA_Q_RANK + MLA_KV_RANK
    kr = (a[:, o_kr:o_kr + MLA_ROPE] * c_ref[...]
          + a[:, o_kr + MLA_ROPE:o_kr + 2 * MLA_ROPE] * s_ref[...]).astype(BF16)
    cos_t = ct_ref[...]
    sin_t = st_ref[...]
    n_nope = MLA_HEADS * MLA_NOPE
    n_rope = MLA_HEADS * MLA_ROPE
    for h in range(MLA_HEADS):
        qt_ref[h, 0:MLA_NOPE, :] = qt[h * MLA_NOPE:(h + 1) * MLA_NOPE, :].astype(BF16)
        r0 = n_nope + h * MLA_ROPE
        qr = qt[r0:r0 + MLA_ROPE, :] * cos_t + qt[r0 + n_rope:r0 + n_rope + MLA_ROPE, :] * sin_t
        qt_ref[h, MLA_NOPE:MLA_QK, :] = qr.astype(BF16)
        k_ref[h, :, 0:MLA_NOPE] = kn[:, h * MLA_NOPE:(h + 1) * MLA_NOPE].astype(BF16)
        k_ref[h, :, MLA_NOPE:MLA_QK] = kr


def _mla_qkv(a, gq, gkv, wqt, wk, wvt, rope):
    m, wa = a.shape
    tm = ROW_TILE
    cos, sin, cos_t, sin_t = rope
    nt = cos.shape[0] // tm
    hv = MLA_HEADS * MLA_V
    full = lambda w: pl.BlockSpec(w.shape, lambda i: (0, 0))
    return pl.pallas_call(
        _mla_qkv_kernel,
        grid=(m // tm,),
        in_specs=[pl.BlockSpec((tm, wa), lambda i: (i, 0)),
                  pl.BlockSpec((1, MLA_Q_RANK), lambda i: (0, 0)),
                  pl.BlockSpec((1, MLA_KV_RANK), lambda i: (0, 0)),
                  full(wqt), full(wk), full(wvt),
                  pl.BlockSpec((tm, MLA_ROPE), lambda i: (i % nt, 0)),
                  pl.BlockSpec((tm, MLA_ROPE), lambda i: (i % nt, 0)),
                  pl.BlockSpec((MLA_ROPE, tm), lambda i: (0, i % nt)),
                  pl.BlockSpec((MLA_ROPE, tm), lambda i: (0, i % nt))],
        out_specs=[pl.BlockSpec((MLA_HEADS, MLA_QK, tm), lambda i: (0, 0, i)),
                   pl.BlockSpec((MLA_HEADS, tm, MLA_QK), lambda i: (0, i, 0)),
                   pl.BlockSpec((hv, tm), lambda i: (0, i))],
        out_shape=[jax.ShapeDtypeStruct((MLA_HEADS, MLA_QK, m), BF16),
                   jax.ShapeDtypeStruct((MLA_HEADS, m, MLA_QK), BF16),
                   jax.ShapeDtypeStruct((hv, m), BF16)],
        compiler_params=_cparams("arbitrary"),
        name="mla_qkv",
    )(a, gq.reshape(1, -1), gkv.reshape(1, -1), wqt, wk, wvt, cos, sin, cos_t, sin_t)


def _flash_t(q, chunks):
    scores = []
    m = None
    for k, _, bias_t in chunks:
        s = jnp.dot(k, q, preferred_element_type=F32)
        if bias_t is not None:
            s = s + bias_t
        scores.append(s)
        mc = s.max(axis=0, keepdims=True)
        m = mc if m is None else jnp.maximum(m, mc)
    den = acc = None
    for s, (_, vt, _) in zip(scores, chunks):
        p = jnp.exp2(s - m)
        ps = p.sum(axis=0, keepdims=True)
        pv = jnp.dot(vt, p.astype(BF16), preferred_element_type=F32)
        den = ps if den is None else den + ps
        acc = pv if acc is None else acc + pv
    return acc / den


def _attn_pipeline(n_tiles, load_q, chunks_of, store_out, s_bufs, m_bufs):
    def scores(q, chunk, s_buf, off, m):
        n, load_k, _, load_bias = chunk
        s = jnp.dot(load_k(), q, preferred_element_type=F32)
        if load_bias is not None:
            s = s + load_bias()
        s_buf[off:off + n, :] = s
        mc = s.max(axis=0, keepdims=True)
        return mc if m is None else jnp.maximum(m, mc)

    def values(chunk, s_buf, off, m, den, acc):
        n, _, load_vt, _ = chunk
        p = jnp.exp2(s_buf[off:off + n, :] - m)
        ps = p.sum(axis=0, keepdims=True)
        pv = jnp.dot(load_vt(), p.astype(BF16), preferred_element_type=F32)
        return (ps if den is None else den + ps), (pv if acc is None else acc + pv)

    q0 = load_q(0)
    m0, off = None, 0
    for chunk in chunks_of(0):
        m0 = scores(q0, chunk, s_bufs[0], off, m0)
        off += chunk[0]
    m_bufs[0][...] = m0

    def step(t, cur, nxt):
        t_next = jnp.minimum(t + 1, n_tiles - 1)
        q_next = load_q(t_next)
        m = m_bufs[cur][...]
        m_next = den = acc = None
        off = 0
        for c_cur, c_next in zip(chunks_of(t), chunks_of(t_next)):
            m_next = scores(q_next, c_next, s_bufs[nxt], off, m_next)
            den, acc = values(c_cur, s_bufs[cur], off, m, den, acc)
            off += c_cur[0]
        m_bufs[nxt][...] = m_next
        store_out(t, acc / den)

    def two_steps(i, carry):
        step(2 * i, 0, 1)
        step(2 * i + 1, 1, 0)
        return carry

    assert n_tiles % 2 == 0
    lax.fori_loop(0, n_tiles // 2, two_steps, 0)


def _mla_attn_kernel(q_ref, kc_ref, kl_ref, vtc_ref, vtl_ref, o_ref, s_a, s_b, m_a, m_b, *, tq, chunk):
    seq = kl_ref.shape[0]
    chunks = [(kc_ref.shape[0], lambda: kc_ref[...], lambda: vtc_ref[...], None)]
    for c0 in range(0, seq, chunk):
        chunks.append((chunk, lambda c0=c0: kl_ref[c0:c0 + chunk, :], lambda c0=c0: vtl_ref[:, c0:c0 + chunk], None))

    def load_q(t):
        return q_ref[:, pl.ds(pl.multiple_of(t * tq, tq), tq)]

    def store_out(t, out_t):
        o_ref[pl.ds(pl.multiple_of(t * tq, tq), tq), :] = out_t.T.astype(o_ref.dtype)

    _attn_pipeline(seq // tq, load_q, lambda t: chunks, store_out, (s_a, s_b), (m_a, m_b))


def _mla_attn(q_l, k_c, k_l, vt_c, vt_l, batch, seq, ctx_len):
    tq = ROW_TILE
    n_keys = seq + ctx_len
    return pl.pallas_call(
        functools.partial(_mla_attn_kernel, tq=tq, chunk=KEY_CHUNK),
        grid=(batch, MLA_HEADS),
        in_specs=[pl.BlockSpec((None, MLA_QK, seq), lambda b, h: (h, 0, b)),
                  pl.BlockSpec((None, ctx_len, MLA_QK), lambda b, h: (h, b, 0)),
                  pl.BlockSpec((None, seq, MLA_QK), lambda b, h: (h, b, 0)),
                  pl.BlockSpec((MLA_V, ctx_len), lambda b, h: (h, b)),
                  pl.BlockSpec((MLA_V, seq), lambda b, h: (h, b))],
        out_specs=pl.BlockSpec((seq, MLA_V), lambda b, h: (b, h)),
        out_shape=jax.ShapeDtypeStruct((batch * seq, MLA_HEADS * MLA_V), BF16),
        scratch_shapes=[pltpu.VMEM((n_keys, tq), F32), pltpu.VMEM((n_keys, tq), F32),
                        pltpu.VMEM((1, tq), F32), pltpu.VMEM((1, tq), F32)],
        compiler_params=_cparams("arbitrary", "arbitrary"),
        name="mla_attn",
    )(q_l, k_c, k_l, vt_c, vt_l)


def _attn_kernel(*refs, plan, nseg):
    q_ref = refs[0]
    k_refs = refs[1:1 + nseg]
    vt_refs = refs[1 + nseg:1 + 2 * nseg]
    o_ref = refs[1 + 2 * nseg]
    chunks = [(k_refs[s][c0:c0 + n, :], vt_refs[s][:, c0:c0 + n], None) for s, c0, n in plan]
    o_ref[...] = _flash_t(q_ref[...], chunks).T.astype(o_ref.dtype)


def _attend(grid, plan, q, q_spec, ks, k_specs, vts, vt_specs, out_shape, out_spec, name):
    nseg = len(ks)
    return pl.pallas_call(
        functools.partial(_attn_kernel, plan=tuple(plan), nseg=nseg),
        grid=grid,
        in_specs=[q_spec] + list(k_specs) + list(vt_specs),
        out_specs=out_spec,
        out_shape=out_shape,
        compiler_params=_cparams(*(["arbitrary"] * len(grid))),
        name=name,
    )(q, *ks, *vts)


def _natten_row_structure(n_rows):
    def block(r0):
        ws = np.clip(r0 - NA_WIN_ROWS // 2, 0, n_rows - NA_KEY_ROWS)
        r = r0 + np.arange(NA_ROWS_PER_STEP)
        start = np.clip(r - NA_WIN_ROWS // 2, 0, n_rows - NA_WIN_ROWS)
        key_row = ws + np.arange(NA_KEY_ROWS)
        kr = key_row[None, :] - start[:, None]
        valid = (kr >= 0) & (kr < NA_WIN_ROWS)
        dr = key_row[None, :] - r[:, None] + (NA_WIN_ROWS - 1)
        return valid, np.where(valid, dr, 0)

    blocks = [block(r0) for r0 in range(0, n_rows, NA_ROWS_PER_STEP)]
    for valid, dr in blocks[2:-1]:
        assert (valid == blocks[1][0]).all() and (dr == blocks[1][1]).all()
    return blocks[0], blocks[1], blocks[-1]


def _natten_kernel(q_ref, k_ref, vt_ref, kc_ref, vtc_ref, rpb_ref, o_ref, bias, s_a, s_b, m_a, m_b, *, structure):
    @pl.when(pl.program_id(1) == 0)
    def _():
        masked = jnp.full((GRID_W, GRID_W), NEG_INF, F32)
        for v, (valid, dr) in enumerate(structure):
            for i in range(NA_KEY_ROWS):
                tiles = [rpb_ref[int(dr[j, i])] if valid[j, i] else masked for j in range(NA_ROWS_PER_STEP)]
                bias[v, i * GRID_W:(i + 1) * GRID_W, :] = jnp.concatenate(tiles, axis=1)

    seq = k_ref.shape[0]
    n_rows = seq // GRID_W
    tq = NA_ROWS_PER_STEP * GRID_W
    n_tiles = seq // tq
    half = NA_KEY_ROWS * GRID_W // 2
    align = NA_KEY_BLOCK * GRID_W

    def chunks_of(t):
        variant = jnp.where(t == 0, 0, jnp.where(t == n_tiles - 1, 2, 1))
        first_row = jnp.clip(t * NA_ROWS_PER_STEP - NA_WIN_ROWS // 2, 0, n_rows - NA_KEY_ROWS)
        chunks = [(kc_ref.shape[0], lambda: kc_ref[...], lambda: vtc_ref[...], None)]
        for c in range(2):
            k0 = pl.multiple_of(first_row * GRID_W + c * half, align)
            chunks.append((half,
                           lambda k0=k0: k_ref[pl.ds(k0, half), :],
                           lambda k0=k0: vt_ref[:, pl.ds(k0, half)],
                           lambda c=c: bias[variant, c * half:(c + 1) * half, :]))
        return chunks

    def load_q(t):
        return q_ref[:, pl.ds(pl.multiple_of(t * tq, tq), tq)]

    def store_out(t, out_t):
        o_ref[pl.ds(pl.multiple_of(t * tq, tq), tq), :] = out_t.T.astype(o_ref.dtype)

    _attn_pipeline(n_tiles, load_q, chunks_of, store_out, (s_a, s_b), (m_a, m_b))


def _natten(k_l, qvt_l, k_c, qvt_c, rpb_tiles, batch, seq, ctx_len):
    n_rows = seq // GRID_W
    tq = NA_ROWS_PER_STEP * GRID_W
    n_keys = NA_KEY_ROWS * GRID_W + ctx_len
    hd = NA_HEAD_DIM
    nh = NA_HEADS
    return pl.pallas_call(
        functools.partial(_natten_kernel, structure=_natten_row_structure(n_rows)),
        grid=(nh, batch),
        in_specs=[pl.BlockSpec((hd, seq), lambda h, b: (h, b)),
                  pl.BlockSpec((seq, hd), lambda h, b: (b, h)),
                  pl.BlockSpec((hd, seq), lambda h, b: (nh + h, b)),
                  pl.BlockSpec((ctx_len, hd), lambda h, b: (b, h)),
                  pl.BlockSpec((hd, ctx_len), lambda h, b: (nh + h, b)),
                  pl.BlockSpec((None,) + rpb_tiles.shape[1:], lambda h, b: (h, 0, 0, 0))],
        out_specs=pl.BlockSpec((seq, hd), lambda h, b: (b, h)),
        out_shape=jax.ShapeDtypeStruct((batch * seq, NA_WIDTH), BF16),
        scratch_shapes=[pltpu.VMEM((3, NA_KEY_ROWS * GRID_W, tq), F32),
                        pltpu.VMEM((n_keys, tq), F32), pltpu.VMEM((n_keys, tq), F32),
                        pltpu.VMEM((1, tq), F32), pltpu.VMEM((1, tq), F32)],
        compiler_params=_cparams("arbitrary", "arbitrary"),
        name="natten",
    )(qvt_l, k_l, qvt_l, k_c, qvt_c, rpb_tiles)


def _natten_rpb_tiles(rpb):
    col = np.arange(GRID_W)
    c_start = np.clip(col - NA_WIN_COLS // 2, 0, GRID_W - NA_WIN_COLS)
    col_in = (col[None, :] >= c_start[:, None]) & (col[None, :] < c_start[:, None] + NA_WIN_COLS)
    dc = np.clip(col[None, :] - col[:, None] + (NA_WIN_COLS - 1), 0, 2 * NA_WIN_COLS - 2)
    onehot = (dc.T[None] == np.arange(2 * NA_WIN_COLS - 1)[:, None, None]).astype(np.float32)
    t = jnp.einsum("hdc,ckq->hdkq", rpb.astype(F32), onehot, precision=lax.Precision.HIGHEST)
    return jnp.where(col_in.T[None, None], t * LOG2E, NEG_INF)


def _softplus(x):
    return jnp.maximum(x, 0.0) + jnp.log1p(jnp.exp(-jnp.abs(x)))


def _rg_kernel(ul_ref, uc_ref, gl_ref, gc_ref, cw_ref, cb_ref, wa_ref, ba_ref, wx_ref, bx_ref, lam_ref,
               ol_ref, oc_ref, upad, ucv, a_s, b_s, h_s, *, t_lat, t_ctx):
    pad = SUBLANES
    cw = cw_ref[...]
    cb = cb_ref[...]
    zeros_pad = jnp.zeros((pad, RG_BS), F32)
    seg_c = t_ctx // SUBLANES + SEG_PAD
    seg_l = t_lat // SUBLANES + SEG_PAD
    off_l = SUBLANES * seg_c

    def conv_stream(u_ref, t_len, dst, chunk):
        upad[0:pad, :] = zeros_pad
        upad[pad:pad + t_len, :] = u_ref[...]
        upad[pad + t_len:2 * pad + t_len, :] = zeros_pad
        for c0 in range(0, t_len, chunk):
            y = cb
            for k in range(RG_CONV):
                y = y + cw[k:k + 1, :] * upad[pad + c0 + k - RG_CONV // 2:pad + c0 + k - RG_CONV // 2 + chunk, :]
            ucv[dst + c0:dst + c0 + chunk, :] = y

    conv_stream(uc_ref, t_ctx, 0, t_ctx)
    conv_stream(ul_ref, t_lat, t_ctx, 512)

    rate = (-RG_C * LOG2E) * _softplus(-lam_ref[...])
    coef_chunk = 256

    def coef_range(src0, dst0, n_chunks):
        def coef(c, carry):
            r0 = pl.multiple_of(src0 + c * coef_chunk, SUBLANES)
            w0 = pl.multiple_of(dst0 + c * coef_chunk, SUBLANES)
            u = ucv[pl.ds(r0, coef_chunk), :]
            ub = u.astype(BF16)
            for d in range(2):
                r = _sigmoid(jnp.dot(ub, wa_ref[d], preferred_element_type=F32) + ba_ref[d:d + 1, :])
                i = _sigmoid(jnp.dot(ub, wx_ref[d], preferred_element_type=F32) + bx_ref[d:d + 1, :])
                a = jnp.exp2(r * rate[d:d + 1, :])
                a_s[d, pl.ds(w0, coef_chunk), :] = a
                b_s[d, pl.ds(w0, coef_chunk), :] = jnp.sqrt((1.0 - a) * (1.0 + a)) * (i * u)
            return carry
        lax.fori_loop(0, n_chunks, coef, 0)

    coef_range(0, 0, t_ctx // coef_chunk)
    coef_range(t_ctx, off_l, t_lat // coef_chunk)
    for d in range(2):
        for lo, hi in ((t_ctx, off_l), (off_l + t_lat, off_l + SUBLANES * seg_l)):
            a_s[d, lo:hi, :] = jnp.ones((hi - lo, RG_BS), F32)
            b_s[d, lo:hi, :] = jnp.zeros((hi - lo, RG_BS), F32)

    sub = lax.broadcasted_iota(jnp.int32, (SUBLANES, RG_BS), 0)

    def scan_stream(off, seg, h0_f, h0_b):
        def ld(ref, d, j):
            return ref[d, pl.ds(off + j, SUBLANES, stride=seg), :]

        def sweep1(j, st):
            hf, pf, hb, pb = st
            jb = seg - 1 - j
            af = ld(a_s, 0, j)
            ab = ld(a_s, 1, jb)
            return (af * hf + ld(b_s, 0, j), af * pf, ab * hb + ld(b_s, 1, jb), ab * pb)

        z = jnp.zeros((SUBLANES, RG_BS), F32)
        o = jnp.ones((SUBLANES, RG_BS), F32)
        hf, pf, hb, pb = lax.fori_loop(0, seg, sweep1, (z, o, z, o), unroll=4)

        start_f = z
        c = h0_f
        for s in range(SUBLANES):
            start_f = jnp.where(sub == s, c, start_f)
            c = pf[s:s + 1, :] * c + hf[s:s + 1, :]
        end_f = c
        start_b = z
        c = h0_b
        for s in range(SUBLANES - 1, -1, -1):
            start_b = jnp.where(sub == s, c, start_b)
            c = pb[s:s + 1, :] * c + hb[s:s + 1, :]
        end_b = c

        def sweep2(j, st):
            hf, hb = st
            jb = seg - 1 - j
            hf = ld(a_s, 0, j) * hf + ld(b_s, 0, j)
            hb = ld(a_s, 1, jb) * hb + ld(b_s, 1, jb)
            h_s[0, pl.ds(off + j, SUBLANES, stride=seg), :] = hf
            h_s[1, pl.ds(off + jb, SUBLANES, stride=seg), :] = hb
            return (hf, hb)

        lax.fori_loop(0, seg, sweep2, (start_f, start_b), unroll=4)
        return end_f, end_b

    zero_row = jnp.zeros((1, RG_BS), F32)
    end_f, end_b = scan_stream(0, seg_c, zero_row, zero_row)
    scan_stream(off_l, seg_l, end_f, end_b)

    oc_ref[...] = (jax.nn.gelu(gc_ref[...]) * (h_s[0, 0:t_ctx, :] + h_s[1, 0:t_ctx, :])).astype(oc_ref.dtype)
    out_chunk = 512

    def emit(c, carry):
        r0 = pl.multiple_of(c * out_chunk, out_chunk)
        rec = h_s[0, pl.ds(off_l + r0, out_chunk), :] + h_s[1, pl.ds(off_l + r0, out_chunk), :]
        ol_ref[pl.ds(r0, out_chunk), :] = (jax.nn.gelu(gl_ref[pl.ds(r0, out_chunk), :]) * rec).astype(ol_ref.dtype)
        return carry

    lax.fori_loop(0, t_lat // out_chunk, emit, 0)


def _rglru(rg_l, rg_c, conv_w, conv_b, wa, ba, wx, bx, lam, batch, seq, ctx_len):
    nb = RG_BLOCKS
    bs = RG_BS
    t_all = seq + ctx_len
    t_pad = t_all + 2 * SUBLANES * SEG_PAD
    blk = lambda b, j: (0, j)
    return pl.pallas_call(
        functools.partial(_rg_kernel, t_lat=seq, t_ctx=ctx_len),
        grid=(batch, nb),
        in_specs=[pl.BlockSpec((seq, bs), lambda b, j: (b, j)),
                  pl.BlockSpec((ctx_len, bs), lambda b, j: (b, j)),
                  pl.BlockSpec((seq, bs), lambda b, j: (b, nb + j)),
                  pl.BlockSpec((ctx_len, bs), lambda b, j: (b, nb + j)),
                  pl.BlockSpec((RG_CONV, bs), blk),
                  pl.BlockSpec((1, bs), blk),
                  pl.BlockSpec((2, None, bs, bs), lambda b, j: (0, j, 0, 0)),
                  pl.BlockSpec((2, bs), blk),
                  pl.BlockSpec((2, None, bs, bs), lambda b, j: (0, j, 0, 0)),
                  pl.BlockSpec((2, bs), blk),
                  pl.BlockSpec((2, bs), blk)],
        out_specs=[pl.BlockSpec((seq, bs), lambda b, j: (b, j)),
                   pl.BlockSpec((ctx_len, bs), lambda b, j: (b, j))],
        out_shape=[jax.ShapeDtypeStruct((batch * seq, RG_WIDTH), BF16),
                   jax.ShapeDtypeStruct((batch * ctx_len, RG_WIDTH), BF16)],
        scratch_shapes=[pltpu.VMEM((seq + 2 * SUBLANES, bs), F32),
                        pltpu.VMEM((t_all, bs), F32),
                        pltpu.VMEM((2, t_pad, bs), F32),
                        pltpu.VMEM((2, t_pad, bs), F32),
                        pltpu.VMEM((2, t_pad, bs), F32)],
        compiler_params=_cparams("arbitrary", "arbitrary"),
        name="rglru",
    )(rg_l, rg_c, rg_l, rg_c, conv_w, conv_b.reshape(1, -1), wa, ba, wx, bx, lam)


def _merge_kernel(m_ref, r_ref, n_ref, gm_ref, gr_ref, gn_ref, wm_ref, wr_ref, wn_ref, o_ref):
    y = gm_ref[...].astype(F32) * jnp.dot(m_ref[...], wm_ref[...], preferred_element_type=F32)
    y = y + gr_ref[...].astype(F32) * jnp.dot(r_ref[...], wr_ref[...], preferred_element_type=F32)
    y = y + gn_ref[...].astype(F32) * jnp.dot(n_ref[...], wn_ref[...], preferred_element_type=F32)
    o_ref[...] = y.astype(o_ref.dtype)


def _merge(mla_o, rg_o, na_o, gates, wm, wr, wn, layer):
    m, k = mla_o.shape
    n = wm.shape[2]
    tm = ROW_TILE
    tn = 1024
    nj = n // tn
    a_spec = pl.BlockSpec((tm, k), lambda j, i: (i, 0))
    w_spec = pl.BlockSpec((None, k, tn), lambda j, i: (layer, 0, j))
    return pl.pallas_call(
        _merge_kernel,
        grid=(nj, m // tm),
        in_specs=[a_spec, a_spec, a_spec,
                  pl.BlockSpec((tm, tn), lambda j, i: (i, j)),
                  pl.BlockSpec((tm, tn), lambda j, i: (i, nj + j)),
                  pl.BlockSpec((tm, tn), lambda j, i: (i, 2 * nj + j)),
                  w_spec, w_spec, w_spec],
        out_specs=pl.BlockSpec((tm, tn), lambda j, i: (i, j)),
        out_shape=jax.ShapeDtypeStruct((m, n), BF16),
        compiler_params=_cparams("arbitrary", "arbitrary"),
        name="merge",
    )(mla_o, rg_o, na_o, gates, gates, gates, wm, wr, wn)


def _proj_res_kernel(y_ref, w_ref, x_ref, g_ref, gate_ref, gn_ref, sh_ref, sc_ref, o_ref, h_ref):
    z = jnp.dot(y_ref[...], w_ref[...], preferred_element_type=F32)
    x_new = x_ref[...] + gate_ref[...] * _rms(z, g_ref[...])
    o_ref[...] = x_new
    h_ref[...] = _modulated(x_new, gn_ref[...], sh_ref[...], sc_ref[...])


def _proj_residual(y, w, layer, x, g, g_next, mods, k_gate, k_shift, k_scale, cond_row):
    m, d = x.shape
    tm = ROW_TILE
    row_spec = pl.BlockSpec((tm, d), lambda i: (i, 0))
    vec_spec = pl.BlockSpec((1, d), lambda i: (0, 0))
    return pl.pallas_call(
        _proj_res_kernel,
        grid=(m // tm,),
        in_specs=[pl.BlockSpec((tm, y.shape[1]), lambda i: (i, 0)),
                  pl.BlockSpec((None,) + w.shape[1:], lambda i: (layer, 0, 0)),
                  row_spec, vec_spec, _mod_spec(k_gate, cond_row),
                  vec_spec, _mod_spec(k_shift, cond_row), _mod_spec(k_scale, cond_row)],
        out_specs=[row_spec, row_spec],
        out_shape=[jax.ShapeDtypeStruct((m, d), F32), jax.ShapeDtypeStruct((m, d), BF16)],
        compiler_params=_cparams("arbitrary"),
        name="out_proj",
    )(y, w, x, g.reshape(1, d), mods, g_next.reshape(1, d), mods, mods)


def _ffn_up_kernel(h_ref, wg_ref, wu_ref, o_ref, wg_s, wu_s):
    @pl.when(pl.program_id(1) == 0)
    def _():
        wg_s[...] = wg_ref[...].astype(BF16)
        wu_s[...] = wu_ref[...].astype(BF16)

    h = h_ref[...]
    g = jnp.dot(h, wg_s[...], preferred_element_type=F32)
    u = jnp.dot(h, wu_s[...], preferred_element_type=F32)
    o_ref[...] = (g * _sigmoid(g) * u).astype(o_ref.dtype)


def _ffn_up(h, wg, wu, layer):
    m, k = h.shape
    n = wg.shape[2]
    tm = ROW_TILE
    tn = 512
    w_spec = pl.BlockSpec((None, k, tn), lambda j, i: (layer, 0, j))
    return pl.pallas_call(
        _ffn_up_kernel,
        grid=(n // tn, m // tm),
        in_specs=[pl.BlockSpec((tm, k), lambda j, i: (i, 0)), w_spec, w_spec],
        out_specs=pl.BlockSpec((tm, tn), lambda j, i: (i, j)),
        out_shape=jax.ShapeDtypeStruct((m, n), BF16),
        scratch_shapes=[pltpu.VMEM((k, tn), BF16), pltpu.VMEM((k, tn), BF16)],
        compiler_params=_cparams("arbitrary", "arbitrary"),
        name="ffn_up",
    )(h, wg, wu)


def _ffn_down_kernel(*refs, with_next):
    a_ref, w_ref, x_ref, g_ref, gate_ref = refs[:5]
    o_ref, acc_ref = (refs[8], refs[10]) if with_next else (refs[5], refs[6])
    kk = pl.program_id(1)

    @pl.when(kk == 0)
    def _():
        acc_ref[...] = jnp.zeros_like(acc_ref)

    acc_ref[...] += jnp.dot(a_ref[...], w_ref[...], preferred_element_type=F32)

    @pl.when(kk == pl.num_programs(1) - 1)
    def _():
        x_new = x_ref[...] + gate_ref[...] * _rms(acc_ref[...], g_ref[...])
        o_ref[...] = x_new
        if with_next:
            gn_ref, sh_ref, sc_ref, h_ref = refs[5], refs[6], refs[7], refs[9]
            h_ref[...] = _modulated(x_new, gn_ref[...], sh_ref[...], sc_ref[...])


def _ffn_down(act, w, layer, x, g, mods, k_gate, cond_row, next_layer=None):
    m, d = x.shape
    kdim = act.shape[1]
    tm = ROW_TILE
    tk = kdim // 4
    row_spec = pl.BlockSpec((tm, d), lambda i, k: (i, 0))
    vec_spec = pl.BlockSpec((1, d), lambda i, k: (0, 0))
    in_specs = [pl.BlockSpec((tm, tk), lambda i, k: (i, k)),
                pl.BlockSpec((None, tk, d), lambda i, k: (layer, k, 0)),
                row_spec, vec_spec, _mod_spec(k_gate, cond_row)]
    args = [act, w, x, g.reshape(1, d), mods]
    out_specs = [row_spec]
    out_shape = [jax.ShapeDtypeStruct((m, d), F32)]
    if next_layer is not None:
        g_next, mods_next = next_layer
        in_specs += [vec_spec, _mod_spec(0, cond_row), _mod_spec(1, cond_row)]
        args += [g_next.reshape(1, d), mods_next, mods_next]
        out_specs.append(row_spec)
        out_shape.append(jax.ShapeDtypeStruct((m, d), BF16))
    outs = pl.pallas_call(
        functools.partial(_ffn_down_kernel, with_next=next_layer is not None),
        grid=(m // tm, kdim // tk),
        in_specs=in_specs,
        out_specs=out_specs,
        out_shape=out_shape,
        scratch_shapes=[pltpu.VMEM((tm, d), F32)],
        compiler_params=_cparams("arbitrary", "arbitrary"),
        name="ffn_down",
    )(*args)
    return (outs[0], outs[1]) if next_layer is not None else (outs[0], None)


def _rope_tables(n_tok):
    t = jnp.arange(n_tok, dtype=jnp.int32)
    row = (t // GRID_W).astype(F32)
    col = (t % GRID_W).astype(F32)
    n_freq = MLA_ROPE // 4
    inv = ROPE_BASE ** (-jnp.arange(n_freq, dtype=F32) / n_freq)
    ang = jnp.stack([row[:, None] * inv, col[:, None] * inv], axis=1)
    cos = jnp.broadcast_to(jnp.cos(ang)[:, :, None, :], (n_tok, 2, 2, n_freq))
    sin = jnp.sin(ang)
    sin = jnp.stack([-sin, sin], axis=2)
    cos, sin = cos.reshape(n_tok, MLA_ROPE), sin.reshape(n_tok, MLA_ROPE)
    return cos, sin, cos.T, sin.T


def _half_swap_perm():
    n_freq = MLA_ROPE // 4
    return np.arange(MLA_ROPE).reshape(2, 2, n_freq)[:, ::-1, :].reshape(MLA_ROPE)


IN_OFFSETS = np.cumsum((0, MLA_Q_RANK, MLA_KV_RANK, MLA_ROPE, RG_WIDTH, RG_WIDTH, NA_WIDTH, NA_WIDTH, NA_WIDTH,
                        D_MODEL, D_MODEL, D_MODEL))


def _w_in_prep_kernel(wt_ref, krs_ref, wa_ref, wrg_ref, wk_ref, wqvt_ref, wgt_ref):
    o = IN_OFFSETS

    def kn(rows):
        return rows.T.astype(BF16)

    wa_ref[:, 0:o[2]] = kn(wt_ref[0:o[2], :])
    wa_ref[:, o[2]:o[2] + 2 * MLA_ROPE] = kn(jnp.concatenate([wt_ref[o[2]:o[3], :], krs_ref[...]], axis=0))
    wrg_ref[...] = kn(wt_ref[o[3]:o[5], :])
    wk_ref[...] = kn(wt_ref[o[6]:o[7], :])
    wqvt_ref[0:NA_WIDTH, :] = (wt_ref[o[5]:o[6], :] * (NA_SCALE * LOG2E)).astype(BF16)
    wqvt_ref[NA_WIDTH:2 * NA_WIDTH, :] = wt_ref[o[7]:o[8], :].astype(BF16)
    wgt_ref[...] = kn(wt_ref[o[8]:o[11], :])


def _w_in_prep(w_in):
    depth, d, n_in = w_in.shape
    o = IN_OFFSETS
    tk = 128
    wt = jnp.swapaxes(w_in, 1, 2)
    krs = wt[:, o[2]:o[3], :][:, _half_swap_perm(), :]
    widths = (o[3] + MLA_ROPE, 2 * RG_WIDTH, NA_WIDTH, 3 * D_MODEL)
    kn_spec = lambda n: pl.BlockSpec((None, tk, n), lambda l, i: (l, i, 0))
    nk_spec = lambda n: pl.BlockSpec((None, n, tk), lambda l, i: (l, 0, i))
    return pl.pallas_call(
        _w_in_prep_kernel,
        grid=(depth, d // tk),
        in_specs=[nk_spec(n_in), nk_spec(MLA_ROPE)],
        out_specs=[kn_spec(widths[0]), kn_spec(widths[1]), kn_spec(widths[2]), nk_spec(2 * NA_WIDTH),
                   kn_spec(widths[3])],
        out_shape=[jax.ShapeDtypeStruct((depth, d, widths[0]), BF16),
                   jax.ShapeDtypeStruct((depth, d, widths[1]), BF16),
                   jax.ShapeDtypeStruct((depth, d, widths[2]), BF16),
                   jax.ShapeDtypeStruct((depth, 2 * NA_WIDTH, d), BF16),
                   jax.ShapeDtypeStruct((depth, d, widths[3]), BF16)],
        compiler_params=_cparams("arbitrary", "arbitrary"),
        name="w_in_prep",
    )(wt, krs)


def _cast_kernel(w_ref, o_ref):
    o_ref[...] = w_ref[...].astype(o_ref.dtype)


def _cast_bf16(w):
    depth, k, n = w.shape
    tr = 256
    spec = pl.BlockSpec((None, tr, n), lambda l, i: (l, i, 0))
    return pl.pallas_call(
        _cast_kernel,
        grid=(depth, k // tr),
        in_specs=[spec],
        out_specs=spec,
        out_shape=jax.ShapeDtypeStruct(w.shape, BF16),
        compiler_params=_cparams("arbitrary", "arbitrary"),
        name="cast_bf16",
    )(w)


def _mla_weights(w_q_b, w_kv_b):
    perm = _half_swap_perm()
    wq = w_q_b.reshape(MLA_Q_RANK, MLA_HEADS, MLA_QK)
    wq_rope = wq[:, :, MLA_NOPE:]
    wq = jnp.concatenate([wq[:, :, :MLA_NOPE].reshape(MLA_Q_RANK, -1),
                          wq_rope.reshape(MLA_Q_RANK, -1),
                          wq_rope[:, :, perm].reshape(MLA_Q_RANK, -1)], axis=1)
    wqt = (wq * (MLA_SCALE * LOG2E)).T.astype(BF16)
    wkv = w_kv_b.reshape(MLA_KV_RANK, MLA_HEADS, MLA_NOPE + MLA_V)
    wk = wkv[:, :, :MLA_NOPE].reshape(MLA_KV_RANK, -1).astype(BF16)
    wvt = wkv[:, :, MLA_NOPE:].reshape(MLA_KV_RANK, -1).T.astype(BF16)
    return wqt, wk, wvt


def kernel(x, c, ctx, c_ctx, w_ada, b_ada, g_mix_pre, g_mix_post, g_ffn_pre, g_ffn_post, w_in, g_q_a, w_q_b, g_kv_a, w_kv_b, w_mla_o, conv_w, conv_b, rg_wa, rg_ba, rg_wx, rg_bx, rg_lambda, w_rg_o, na_rpb, w_na_o, w_out, w_ffn_gate, w_ffn_up, w_ffn_down):
    batch, seq, d = x.shape
    ctx_len = ctx.shape[1]
    depth = w_ada.shape[0]
    n_rows = seq // GRID_W
    assert d == D_MODEL and batch + 1 <= COND_ROWS
    assert seq % ROW_TILE == 0 and (batch * ctx_len) % ROW_TILE == 0 and seq % KEY_CHUNK == 0
    assert seq % GRID_W == 0 and n_rows % NA_ROWS_PER_STEP == 0 and n_rows >= 2 * NA_KEY_ROWS
    lat_row = lambda i: i // (seq // ROW_TILE)
    ctx_row = lambda i: batch
    hq, hv = MLA_HEADS, MLA_HEADS * MLA_V
    nh, hd = NA_HEADS, NA_HEAD_DIM

    cond = jnp.zeros((COND_ROWS, d), F32).at[:batch].set(c).at[batch].set(c_ctx)
    mods = _ada(cond, w_ada, b_ada).reshape(depth, COND_ROWS, 6, 1, d)

    rope_l = _rope_tables(seq)
    ones_c, zeros_c = jnp.ones((batch * ctx_len, MLA_ROPE), F32), jnp.zeros((batch * ctx_len, MLA_ROPE), F32)
    rope_c = (ones_c, zeros_c, ones_c.T, zeros_c.T)

    w_a, w_rg, w_na_k, w_na_qvt, w_gt = _w_in_prep(w_in)
    wm, wr, wn, wo, wfd = (_cast_bf16(w) for w in (w_mla_o, w_rg_o, w_na_o, w_out, w_ffn_down))

    xl = x.reshape(batch * seq, d)
    xc = ctx.reshape(batch * ctx_len, d)
    ctx_plan = [(0, 0, ctx_len)]
    hl = _modulate(xl, g_mix_pre[0], mods[0], 0, 1, lat_row)
    hc = _modulate(xc, g_mix_pre[0], mods[0], 0, 1, ctx_row)

    for l in range(depth):
        last = l == depth - 1
        md = mods[l]
        next_layer = None if last else (g_mix_pre[l + 1], mods[l + 1])
        wqt, wk, wvt = _mla_weights(w_q_b[l], w_kv_b[l])

        a_l = _matmul(hl, w_a, l, F32, w_a.shape[2], name="proj_mla")
        a_c = _matmul(hc, w_a, l, F32, w_a.shape[2], name="proj_mla")
        rg_l = _matmul(hl, w_rg, l, F32, 1024, name="proj_rg")
        rg_c = _matmul(hc, w_rg, l, F32, 1024, name="proj_rg")
        nk_l = _matmul(hl, w_na_k, l, BF16, 1024, name="proj_na_k")
        nk_c = _matmul(hc, w_na_k, l, BF16, 1024, name="proj_na_k")
        nqvt_l = _matmul_nt(w_na_qvt, l, hl, BF16, 1024, name="proj_na_qvt")
        nqvt_c = _matmul_nt(w_na_qvt, l, hc, BF16, 1024, name="proj_na_qvt")
        gt_l = _matmul(hl, w_gt, l, BF16, 1024, act="sigmoid", name="proj_gate")

        qt_l, k_l, vt_l = _mla_qkv(a_l, g_q_a[l], g_kv_a[l], wqt, wk, wvt, rope_l)
        qt_c, k_c, vt_c = _mla_qkv(a_c, g_q_a[l], g_kv_a[l], wqt, wk, wvt, rope_c)
        mla_l = _mla_attn(qt_l, k_c, k_l, vt_c, vt_l, batch, seq, ctx_len)

        rgo_l, rgo_c = _rglru(rg_l, rg_c, conv_w[l], conv_b[l], rg_wa[l].astype(BF16), rg_ba[l],
                              rg_wx[l].astype(BF16), rg_bx[l], rg_lambda[l], batch, seq, ctx_len)

        nao_l = _natten(nk_l, nqvt_l, nk_c, nqvt_c, _natten_rpb_tiles(na_rpb[l]), batch, seq, ctx_len)

        def tail(xs, mla_o, rg_o, na_o, gates, cond_row):
            y = _merge(mla_o, rg_o, na_o, gates, wm, wr, wn, l)
            xs, h2 = _proj_residual(y, wo, l, xs, g_mix_post[l], g_ffn_pre[l], md, 2, 3, 4, cond_row)
            act = _ffn_up(h2, w_ffn_gate, w_ffn_up, l)
            return _ffn_down(act, wfd, l, xs, g_ffn_post[l], md, 5, cond_row, next_layer)

        xl, hl = tail(xl, mla_l, rgo_l, nao_l, gt_l, lat_row)

        if not last:
            gt_c = _matmul(hc, w_gt, l, BF16, 1024, act="sigmoid", name="proj_gate")
            mla_c = _attend(
                (batch, hq), ctx_plan, qt_c,
                pl.BlockSpec((None, MLA_QK, ctx_len), lambda b, h: (h, 0, b)),
                [k_c], [pl.BlockSpec((None, ctx_len, MLA_QK), lambda b, h: (h, b, 0))],
                [vt_c], [pl.BlockSpec((MLA_V, ctx_len), lambda b, h: (h, b))],
                jax.ShapeDtypeStruct((batch * ctx_len, hv), BF16),
                pl.BlockSpec((ctx_len, MLA_V), lambda b, h: (b, h)),
                "mla_attn_ctx")
            nao_c = _attend(
                (batch, nh), ctx_plan, nqvt_c,
                pl.BlockSpec((hd, ctx_len), lambda b, h: (h, b)),
                [nk_c], [pl.BlockSpec((ctx_len, hd), lambda b, h: (b, h))],
                [nqvt_c], [pl.BlockSpec((hd, ctx_len), lambda b, h: (nh + h, b))],
                jax.ShapeDtypeStruct((batch * ctx_len, NA_WIDTH), BF16),
                pl.BlockSpec((ctx_len, hd), lambda b, h: (b, h)),
                "na_attn_ctx")
            xc, hc = tail(xc, mla_c, rgo_c, nao_c, gt_c, ctx_row)

    return xl.reshape(batch, seq, d)
```
